```python
import jax
import jax.numpy as jnp
from jax import lax
import numpy as np

D_MODEL = 2048
BATCH = 2
SEQ = 16384
DEPTH = 2

HEAD_DIM = 128
CONV_CH = 512
CONV_K = 31
NSA_HEADS = 6
NSA_KV = 2
NSA_HPG = NSA_HEADS // NSA_KV
CMP_BLK = 32
CMP_STRIDE = 16
CMP_HIDDEN = 256
SEL_BLK = 64
SEL_TOP = 16
CMP_PER_SEL = SEL_BLK // CMP_STRIDE
NSA_WIN = 512
NSA_QBLK = 128
SWA_HEADS = 6
SWA_KV = 2
SWA_HPG = SWA_HEADS // SWA_KV
SWA_WIN = 128
SWA_BLK = 128
MIX_WIDTH = CONV_CH + (NSA_HEADS + SWA_HEADS) * HEAD_DIM
N_OUT_GROUPS = MIX_WIDTH // HEAD_DIM
IN_WIDTH = 2 * CONV_CH + NSA_HEADS * HEAD_DIM + 6 * NSA_KV * HEAD_DIM + 3 * NSA_HEADS + (SWA_HEADS + 2 * SWA_KV) * HEAD_DIM
FFN_DENSE = 5632
N_EXPERTS = 8
TOP_K = 2
FFN_EXPERT = 7168
MOE_BLK = 512
N_ADA = 6
N_DENSE_LAYERS = (DEPTH + 1) // 2
N_MOE_LAYERS = DEPTH // 2
NORM_EPS = 1e-6
LN_EPS = 1e-5
F32 = jnp.float32

kernel_name = 'hybrid_conv_nsa_swasink_moe_trunk'


def rms_norm(x, g):
    xf = x.astype(F32)
    y = xf * lax.rsqrt(jnp.mean(xf * xf, axis=-1, keepdims=True) + NORM_EPS)
    return (y * g.astype(F32)).astype(x.dtype)


def layer_norm(x, g, bias):
    xf = x.astype(F32)
    mu = jnp.mean(xf, axis=-1, keepdims=True)
    var = jnp.mean(jnp.square(xf - mu), axis=-1, keepdims=True)
    return ((xf - mu) * lax.rsqrt(var + LN_EPS) * g.astype(F32) + bias.astype(F32)).astype(x.dtype)


def masked_softmax(s, mask):
    s = jnp.where(mask, s, -jnp.inf)
    m = jnp.max(s, axis=-1, keepdims=True)
    m = jnp.where(jnp.isfinite(m), m, 0.0)
    p = jnp.exp(s - m)
    return p / jnp.maximum(jnp.sum(p, axis=-1, keepdims=True), 1e-30)


def split_columns(proj):
    sizes = [2 * CONV_CH, NSA_HEADS * HEAD_DIM] + [NSA_KV * HEAD_DIM] * 6 + [3 * NSA_HEADS, SWA_HEADS * HEAD_DIM, SWA_KV * HEAD_DIM, SWA_KV * HEAD_DIM]
    return jnp.split(proj, np.cumsum(sizes)[:-1].tolist(), axis=-1)


def conformer_conv(u, w_dw, b_dw, ln_g, ln_b, w_pw):
    val, gate = jnp.split(u, 2, axis=-1)
    z = val * jax.nn.sigmoid(gate)
    z = lax.conv_general_dilated(z, w_dw.astype(z.dtype)[:, None, :], (1,), [(CONV_K - 1, 0)],
                                 dimension_numbers=('NWC', 'WIO', 'NWC'),
                                 feature_group_count=CONV_CH) + b_dw
    z = jax.nn.silu(layer_norm(z, ln_g, ln_b))
    return z @ w_pw


def compress(z, pe, w1, b1, w2):
    b, t, g, dh = z.shape
    ch = z.reshape(b, t // CMP_STRIDE, CMP_STRIDE, g, dh)
    blocks = jnp.concatenate([ch[:, :-1], ch[:, 1:]], axis=2)
    blocks = blocks + pe[None, None, :, None, :]
    nc = blocks.shape[1]
    flat = blocks.transpose(0, 1, 3, 2, 4).reshape(b, nc, g, CMP_BLK * dh)
    return jax.nn.gelu(flat @ w1 + b1) @ w2


def nsa_attention(q, k_cmp, v_cmp, k_slc, v_slc, k_win, v_win, gates):
    b, t = q.shape[:2]
    n_cmp = k_cmp.shape[1]
    n_sel = t // SEL_BLK
    n_top = min(SEL_TOP, n_sel)
    n_qblk = t // NSA_QBLK
    scale = HEAD_DIM ** -0.5
    qg = q.reshape(b, t, NSA_KV, NSA_HPG, HEAD_DIM)
    gg = gates.reshape(b, t, NSA_KV, NSA_HPG, 3)
    kc = k_cmp.transpose(0, 2, 1, 3)
    vc = v_cmp.transpose(0, 2, 1, 3)
    ks = k_slc.reshape(b, n_sel, SEL_BLK, NSA_KV, HEAD_DIM).transpose(0, 3, 1, 2, 4)
    vs = v_slc.reshape(b, n_sel, SEL_BLK, NSA_KV, HEAD_DIM).transpose(0, 3, 1, 2, 4)
    pad_w = ((0, 0), (NSA_WIN, 0), (0, 0), (0, 0))
    kw = jnp.pad(k_win, pad_w)
    vw = jnp.pad(v_win, pad_w)
    cmp_end = jnp.arange(n_cmp) * CMP_STRIDE + (CMP_BLK - 1)
    sel_ids = jnp.arange(n_sel)
    bi = jnp.arange(b)[:, None, None, None]
    gi = jnp.arange(NSA_KV)[None, :, None, None]
    imp_pad = ((0, 0), (0, 0), (0, 0), (1, CMP_PER_SEL * (n_sel + 1) - 1 - n_cmp))

    def one_block(blk):
        start = blk * NSA_QBLK
        qb = lax.dynamic_slice_in_dim(qg, start, NSA_QBLK, axis=1)
        gb = lax.dynamic_slice_in_dim(gg, start, NSA_QBLK, axis=1)
        tpos = start + jnp.arange(NSA_QBLK)
        s = jnp.einsum('bqghd,bgcd->bghqc', qb, kc).astype(F32) * scale
        p_cmp = masked_softmax(s, cmp_end[None, :] <= tpos[:, None])
        o_cmp = jnp.einsum('bghqc,bgcd->bqghd', p_cmp.astype(vc.dtype), vc)
        imp = jnp.pad(jnp.sum(p_cmp, axis=2), imp_pad)
        imp_sel = (imp[..., :CMP_PER_SEL * n_sel].reshape(b, NSA_KV, NSA_QBLK, n_sel, CMP_PER_SEL).sum(-1)
                   + imp[..., CMP_PER_SEL::CMP_PER_SEL])
        cur = (tpos // SEL_BLK)[:, None]
        forced = (sel_ids == 0) | (sel_ids == cur) | (sel_ids == cur - 1)
        score = jnp.where(forced, jnp.inf, jnp.where(sel_ids <= cur, imp_sel, -jnp.inf))
        top_s, top_i = lax.top_k(score, n_top)
        kg = ks[bi, gi, top_i]
        vg = vs[bi, gi, top_i].reshape(b, NSA_KV, NSA_QBLK, n_top * SEL_BLK, HEAD_DIM)
        kpos = top_i[..., None] * SEL_BLK + jnp.arange(SEL_BLK)
        sel_mask = (kpos <= tpos[:, None, None]) & (top_s > -jnp.inf)[..., None]
        s = jnp.einsum('bqghd,bgqnkd->bghqnk', qb, kg).astype(F32) * scale
        p_slc = masked_softmax(s.reshape(b, NSA_KV, NSA_HPG, NSA_QBLK, n_top * SEL_BLK),
                               sel_mask.reshape(b, NSA_KV, 1, NSA_QBLK, n_top * SEL_BLK))
        o_slc = jnp.einsum('bghqm,bgqmd->bqghd', p_slc.astype(vg.dtype), vg)
        kwb = lax.dynamic_slice_in_dim(kw, start, NSA_QBLK + NSA_WIN, axis=1)
        vwb = lax.dynamic_slice_in_dim(vw, start, NSA_QBLK + NSA_WIN, axis=1)
        kpos_w = start - NSA_WIN + jnp.arange(NSA_QBLK + NSA_WIN)
        rel = tpos[:, None] - kpos_w[None, :]
        win_mask = (rel >= 0) & (rel < NSA_WIN) & (kpos_w[None, :] >= 0)
        s = jnp.einsum('bqghd,bkgd->bghqk', qb, kwb).astype(F32) * scale
        p_win = masked_softmax(s, win_mask)
        o_win = jnp.einsum('bghqk,bkgd->bqghd', p_win.astype(vwb.dtype), vwb)
        return gb[..., 0:1] * o_cmp + gb[..., 1:2] * o_slc + gb[..., 2:3] * o_win

    out = lax.map(one_block, jnp.arange(n_qblk))
    return jnp.moveaxis(out, 0, 1).reshape(b, t, NSA_HEADS * HEAD_DIM)


def swa_sink_attention(q, k, v, sinks):
    b, t = q.shape[:2]
    nb = t // SWA_BLK
    scale = HEAD_DIM ** -0.5
    qb = q.reshape(b, nb, SWA_BLK, SWA_KV, SWA_HPG, HEAD_DIM)

    def band(z):
        zp = jnp.pad(z, ((0, 0), (SWA_BLK, 0), (0, 0), (0, 0))).reshape(b, nb + 1, SWA_BLK, SWA_KV, HEAD_DIM)
        return jnp.concatenate([zp[:, :-1], zp[:, 1:]], axis=2)

    kb, vb = band(k), band(v)
    rel = jnp.arange(SWA_BLK)[:, None] + SWA_BLK - jnp.arange(2 * SWA_BLK)[None, :]
    kpos = jnp.arange(nb)[:, None] * SWA_BLK - SWA_BLK + jnp.arange(2 * SWA_BLK)[None, :]
    mask = ((rel >= 0) & (rel < SWA_WIN))[None] & (kpos >= 0)[:, None, :]
    s = jnp.einsum('bnqghd,bnkgd->bnghqk', qb, kb).astype(F32) * scale
    s = jnp.where(mask[None, :, None, None], s, -jnp.inf)
    sink = sinks.astype(F32).reshape(SWA_KV, SWA_HPG)[None, None, :, :, None, None]
    m = jnp.maximum(jnp.max(s, axis=-1, keepdims=True), sink)
    p = jnp.exp(s - m)
    p = p / (jnp.sum(p, axis=-1, keepdims=True) + jnp.exp(sink - m))
    o = jnp.einsum('bnghqk,bnkgd->bnqghd', p.astype(vb.dtype), vb)
    return o.reshape(b, t, SWA_HEADS * HEAD_DIM)


def hybrid_mixer(h, w_in, conv_dw_w, conv_dw_b, conv_ln_g, conv_ln_b, conv_pw_w, cmp_pe, cmp_w1, cmp_b1,
                 cmp_w2, nsa_q_g, nsa_k_g, swa_q_g, swa_k_g, swa_sinks, grp_out_g, w_out):
    b, t, _ = h.shape
    (u_conv, q_n, kc_raw, vc_raw, ks_raw, vs_raw, kw_raw, vw_raw, g_n,
     q_s, k_s, v_s) = split_columns(h @ w_in)

    def heads(z, n):
        return z.reshape(b, t, n, HEAD_DIM)

    y_a = conformer_conv(u_conv, conv_dw_w, conv_dw_b, conv_ln_g, conv_ln_b, conv_pw_w)
    q_nsa = rms_norm(heads(q_n, NSA_HEADS), nsa_q_g)
    k_cmp = rms_norm(compress(heads(kc_raw, NSA_KV), cmp_pe[0], cmp_w1[0], cmp_b1[0], cmp_w2[0]), nsa_k_g[0])
    v_cmp = compress(heads(vc_raw, NSA_KV), cmp_pe[1], cmp_w1[1], cmp_b1[1], cmp_w2[1])
    k_slc = rms_norm(heads(ks_raw, NSA_KV), nsa_k_g[1])
    k_win = rms_norm(heads(kw_raw, NSA_KV), nsa_k_g[2])
    gates = jax.nn.sigmoid(g_n.reshape(b, t, NSA_HEADS, 3))
    y_b = nsa_attention(q_nsa, k_cmp, v_cmp, k_slc, heads(vs_raw, NSA_KV), k_win, heads(vw_raw, NSA_KV), gates)
    y_c = swa_sink_attention(rms_norm(heads(q_s, SWA_HEADS), swa_q_g), rms_norm(heads(k_s, SWA_KV), swa_k_g),
                             heads(v_s, SWA_KV), swa_sinks)
    y = jnp.concatenate([y_a, y_b, y_c], axis=-1).reshape(b, t, N_OUT_GROUPS, HEAD_DIM)
    y = rms_norm(y, grp_out_g.reshape(N_OUT_GROUPS, HEAD_DIM)).reshape(b, t, MIX_WIDTH)
    return y @ w_out


def swiglu(h, w_gate, w_up, w_down):
    return (jax.nn.silu(h @ w_gate) * (h @ w_up)) @ w_down


def moe_swiglu(h, w_router, w_gate, w_up, w_down):
    n_tok, d = h.shape
    logits = jnp.dot(h.astype(F32), w_router.astype(F32))
    top_logit, top_idx = lax.top_k(logits, TOP_K)
    top_w = jax.nn.softmax(top_logit, axis=-1).astype(h.dtype)
    expert_of = top_idx.reshape(-1)
    token_of = jnp.repeat(jnp.arange(n_tok), TOP_K)
    weight_of = top_w.reshape(-1)
    order = jnp.argsort(expert_of)
    e_sorted = expert_of[order]
    tok_sorted = token_of[order]
    counts = jnp.zeros((N_EXPERTS,), jnp.int32).at[expert_of].add(1)
    starts = jnp.cumsum(counts) - counts
    padded = (counts + MOE_BLK - 1) // MOE_BLK * MOE_BLK
    pad_starts = jnp.cumsum(padded) - padded
    pad_ends = pad_starts + padded
    dest = pad_starts[e_sorted] + (jnp.arange(n_tok * TOP_K) - starts[e_sorted])
    n_blocks = -(-(n_tok * TOP_K) // MOE_BLK) + N_EXPERTS
    x_buf = jnp.zeros((n_blocks * MOE_BLK, d), h.dtype).at[dest].set(h[tok_sorted])
    block_start = jnp.arange(n_blocks) * MOE_BLK
    block_expert = jnp.minimum(jnp.sum(block_start[:, None] >= pad_ends[None, :], axis=1), N_EXPERTS - 1)

    def expert_block(args):
        xb, e = args
        return swiglu(xb, w_gate[e], w_up[e], w_down[e])

    y_buf = lax.map(expert_block, (x_buf.reshape(n_blocks, MOE_BLK, d), block_expert)).reshape(-1, d)
    return jnp.zeros((n_tok, d), h.dtype).at[tok_sorted].add(y_buf[dest] * weight_of[order][:, None])


def setup_inputs(seed: int = 0) -> dict:
    key = jax.random.key(seed)
    keys = iter(jax.random.split(key, 32))

    def nrm(shape, scale):
        return jax.random.normal(next(keys), shape, jnp.float32) * scale

    D, L, LD, LM = D_MODEL, DEPTH, N_DENSE_LAYERS, N_MOE_LAYERS
    return {
        'x': nrm((BATCH, SEQ, D), 1.0),
        'c': nrm((BATCH, D), 1.0),
        'norm_mix_g': 1.0 + nrm((L, D), 0.02),
        'norm_ffn_g': 1.0 + nrm((L, D), 0.02),
        'w_ada': nrm((L, D, N_ADA * D), 0.5 * D ** -0.5),
        'b_ada': nrm((L, N_ADA * D), 0.02),
        'w_in': nrm((L, D, IN_WIDTH), D ** -0.5),
        'conv_dw_w': nrm((L, CONV_K, CONV_CH), CONV_K ** -0.5),
        'conv_dw_b': nrm((L, CONV_CH), 0.02),
        'conv_ln_g': 1.0 + nrm((L, CONV_CH), 0.02),
        'conv_ln_b': nrm((L, CONV_CH), 0.02),
        'conv_pw_w': nrm((L, CONV_CH, CONV_CH), CONV_CH ** -0.5),
        'cmp_pe': nrm((L, 2, CMP_BLK, HEAD_DIM), 0.1),
        'cmp_w1': nrm((L, 2, CMP_BLK * HEAD_DIM, CMP_HIDDEN), (CMP_BLK * HEAD_DIM) ** -0.5),
        'cmp_b1': nrm((L, 2, CMP_HIDDEN), 0.02),
        'cmp_w2': nrm((L, 2, CMP_HIDDEN, HEAD_DIM), CMP_HIDDEN ** -0.5),
        'nsa_q_g': 1.0 + nrm((L, HEAD_DIM), 0.02),
        'nsa_k_g': 1.0 + nrm((L, 3, HEAD_DIM), 0.02),
        'swa_q_g': 1.0 + nrm((L, HEAD_DIM), 0.02),
        'swa_k_g': 1.0 + nrm((L, HEAD_DIM), 0.02),
        'swa_sinks': nrm((L, SWA_HEADS), 0.5),
        'grp_out_g': 1.0 + nrm((L, MIX_WIDTH), 0.02),
        'w_out': nrm((L, MIX_WIDTH, D), MIX_WIDTH ** -0.5),
        'ffn_w_gate': nrm((LD, D, FFN_DENSE), D ** -0.5),
        'ffn_w_up': nrm((LD, D, FFN_DENSE), D ** -0.5),
        'ffn_w_down': nrm((LD, FFN_DENSE, D), FFN_DENSE ** -0.5),
        'moe_router': nrm((LM, D, N_EXPERTS), D ** -0.5),
        'moe_w_gate': nrm((LM, N_EXPERTS, D, FFN_EXPERT), D ** -0.5),
        'moe_w_up': nrm((LM, N_EXPERTS, D, FFN_EXPERT), D ** -0.5),
        'moe_w_down': nrm((LM, N_EXPERTS, FFN_EXPERT, D), FFN_EXPERT ** -0.5),
    }


def reference(x, c, norm_mix_g, norm_ffn_g, w_ada, b_ada, w_in, conv_dw_w, conv_dw_b, conv_ln_g, conv_ln_b,
              conv_pw_w, cmp_pe, cmp_w1, cmp_b1, cmp_w2, nsa_q_g, nsa_k_g, swa_q_g, swa_k_g, swa_sinks,
              grp_out_g, w_out, ffn_w_gate, ffn_w_up, ffn_w_down, moe_router, moe_w_gate, moe_w_up, moe_w_down):
    b, t, d = x.shape
    c_act = jax.nn.silu(c)
    for i in range(DEPTH):
        mod = (c_act @ w_ada[i] + b_ada[i])[:, None, :]
        sh_m, sc_m, g_m, sh_f, sc_f, g_f = jnp.split(mod, N_ADA, axis=-1)
        h = rms_norm(x, norm_mix_g[i]) * (1.0 + sc_m) + sh_m
        x = x + g_m * hybrid_mixer(h, w_in[i], conv_dw_w[i], conv_dw_b[i], conv_ln_g[i], conv_ln_b[i],
                                   conv_pw_w[i], cmp_pe[i], cmp_w1[i], cmp_b1[i], cmp_w2[i], nsa_q_g[i],
                                   nsa_k_g[i], swa_q_g[i], swa_k_g[i], swa_sinks[i], grp_out_g[i], w_out[i])
        h = rms_norm(x, norm_ffn_g[i]) * (1.0 + sc_f) + sh_f
        j = i // 2
        if i % 2 == 0:
            f = swiglu(h, ffn_w_gate[j], ffn_w_up[j], ffn_w_down[j])
        else:
            f = moe_swiglu(h.reshape(b * t, d), moe_router[j], moe_w_gate[j], moe_w_up[j],
                           moe_w_down[j]).reshape(b, t, d)
        x = x + g_f * f
    return x
```

```python
import functools

import jax
import jax.numpy as jnp
from jax import lax
from jax.experimental import pallas as pl
from jax.experimental.pallas import tpu as pltpu

F32 = jnp.float32
BF16 = jnp.bfloat16
HIGHEST = lax.Precision.HIGHEST

HEAD_DIM = 128
CONV_CH = 512
CONV_K = 31
NSA_HEADS = 6
NSA_KV = 2
NSA_HPG = NSA_HEADS // NSA_KV
CMP_BLK = 32
CMP_STRIDE = 16
CMP_HIDDEN = 256
SEL_BLK = 64
SEL_TOP = 16
CMP_PER_SEL = SEL_BLK // CMP_STRIDE
NSA_WIN = 512
SWA_HEADS = 6
SWA_KV = 2
SWA_HPG = SWA_HEADS // SWA_KV
SWA_WIN = 128
N_EXPERTS = 8
TOP_K = 2
N_ADA = 6
NORM_EPS = 1e-6
LN_EPS = 1e-5

LANE = 128
SUBLANE = 8
VMEM_LIMIT = 56 * 1024 * 1024

QBLK = 256
KVT = 256
MOE_TM = 1024
NEG = -1e30


def _cparams(sem):
    return pltpu.CompilerParams(dimension_semantics=sem, vmem_limit_bytes=VMEM_LIMIT)


def _mod_rmsnorm(x, g, sc, sh):
    ms = jnp.mean(x * x, axis=-1, keepdims=True)
    y = x * lax.rsqrt(ms + NORM_EPS) * g
    return y * (1.0 + sc) + sh


def _adaln_kernel(c_ref, w_ref, b_ref, o_ref):
    c = c_ref[...]
    ca = c * jax.nn.sigmoid(c)
    o_ref[0] = jnp.dot(ca, w_ref[0], precision=HIGHEST, preferred_element_type=F32) + b_ref[0]


def _adaln(c, w_ada, b_ada):
    n_layers, d, width = w_ada.shape
    b = c.shape[0]
    rows = -(-b // SUBLANE) * SUBLANE
    c_pad = jnp.zeros((rows, d), F32).at[:b].set(c)
    tn = 1024
    out = pl.pallas_call(
        _adaln_kernel,
        out_shape=jax.ShapeDtypeStruct((n_layers, rows, width), F32),
        grid=(n_layers, width // tn),
        in_specs=[
            pl.BlockSpec((rows, d), lambda l, j: (0, 0)),
            pl.BlockSpec((1, d, tn), lambda l, j: (l, 0, j)),
            pl.BlockSpec((1, 1, tn), lambda l, j: (l, 0, j)),
        ],
        out_specs=pl.BlockSpec((1, rows, tn), lambda l, j: (l, 0, j)),
        compiler_params=_cparams(("arbitrary", "arbitrary")),
        name="adaln",
    )(c_pad, w_ada, b_ada.reshape(n_layers, 1, width))
    return out[:, :b].reshape(n_layers, b, N_ADA, d)


def _inproj_kernel(x_ref, g_ref, sc_ref, sh_ref, w_ref, gain_ref, flag_ref, o_ref, h_ref, *, epilogue):
    @pl.when(pl.program_id(1) == 0)
    def _():
        h_ref[...] = _mod_rmsnorm(x_ref[...], g_ref[...], sc_ref[0], sh_ref[0]).astype(BF16)

    y = jnp.dot(h_ref[...], w_ref[...], preferred_element_type=F32)
    if epilogue:
        gain = gain_ref[...]
        flag = flag_ref[...]
        for hh in range(y.shape[1] // HEAD_DIM):
            cs = slice(hh * HEAD_DIM, (hh + 1) * HEAD_DIM)
            yh = y[:, cs]
            ms = jnp.mean(yh * yh, axis=-1, keepdims=True)
            yn = yh * lax.rsqrt(ms + NORM_EPS) * gain[:, cs]
            o_ref[:, cs] = jnp.where(flag[:, cs] > 0.0, yn, yh).astype(o_ref.dtype)
    else:
        o_ref[...] = y.astype(o_ref.dtype)


def _inproj(x2, g, sc, sh, w, gain, flag, *, rows_per_batch, tm, tn, out_dtype, epilogue, name):
    n, d = x2.shape
    width = w.shape[1]
    bpb = rows_per_batch // tm
    return pl.pallas_call(
        functools.partial(_inproj_kernel, epilogue=epilogue),
        out_shape=jax.ShapeDtypeStruct((n, width), out_dtype),
        grid=(n // tm, width // tn),
        in_specs=[
            pl.BlockSpec((tm, d), lambda i, j: (i, 0)),
            pl.BlockSpec((1, d), lambda i, j: (0, 0)),
            pl.BlockSpec((1, 1, d), lambda i, j: (i // bpb, 0, 0)),
            pl.BlockSpec((1, 1, d), lambda i, j: (i // bpb, 0, 0)),
            pl.BlockSpec((d, tn), lambda i, j: (0, j)),
            pl.BlockSpec((1, tn), lambda i, j: (0, j)),
            pl.BlockSpec((1, tn), lambda i, j: (0, j)),
        ],
        out_specs=pl.BlockSpec((tm, tn), lambda i, j: (i, j)),
        scratch_shapes=[pltpu.VMEM((tm, d), BF16)],
        compiler_params=_cparams(("arbitrary", "arbitrary")),
        name=name,
    )(x2, g, sc, sh, w, gain, flag)


CONV_TT = 256
CONV_HALO = 32
CONV_CHUNK = 32


def _conv_kernel(u_ref, halo_ref, wdw_ref, bdw_ref, lng_ref, lnb_ref, wpw_ref, o_ref, z_ref, c_ref):
    t = pl.program_id(1)
    u = u_ref[0]
    z_ref[CONV_HALO:, :] = u[:, :CONV_CH] * jax.nn.sigmoid(u[:, CONV_CH:])
    hu = halo_ref[0]
    zh = hu[:, :CONV_CH] * jax.nn.sigmoid(hu[:, CONV_CH:])
    z_ref[:CONV_HALO, :] = jnp.where(t == 0, 0.0, zh)
    wdw = wdw_ref[...]
    off = CONV_HALO - (CONV_K - 1)
    for c in range(CONV_TT // CONV_CHUNK):
        acc = jnp.zeros((CONV_CHUNK, CONV_CH), F32) + bdw_ref[...]
        for k in range(CONV_K):
            lo = c * CONV_CHUNK + off + k
            acc = acc + wdw[k:k + 1, :] * z_ref[lo:lo + CONV_CHUNK, :]
        c_ref[c * CONV_CHUNK:(c + 1) * CONV_CHUNK, :] = acc
    y = c_ref[...]
    mu = jnp.mean(y, axis=-1, keepdims=True)
    var = jnp.mean(jnp.square(y - mu), axis=-1, keepdims=True)
    y = (y - mu) * lax.rsqrt(var + LN_EPS) * lng_ref[...] + lnb_ref[...]
    y = y * jax.nn.sigmoid(y)
    o_ref[0] = jnp.dot(y.astype(BF16), wpw_ref[...], preferred_element_type=F32)


def _conformer_conv(pa, w_dw, b_dw, ln_g, ln_b, w_pw):
    b, t, _ = pa.shape
    hb = CONV_TT // CONV_HALO
    return pl.pallas_call(
        _conv_kernel,
        out_shape=jax.ShapeDtypeStruct((b, t, CONV_CH), F32),
        grid=(b, t // CONV_TT),
        in_specs=[
            pl.BlockSpec((1, CONV_TT, 2 * CONV_CH), lambda bi, ti: (bi, ti, 0)),
            pl.BlockSpec((1, CONV_HALO, 2 * CONV_CH), lambda bi, ti: (bi, jnp.maximum(ti * hb - 1, 0), 0)),
            pl.BlockSpec((CONV_K, CONV_CH), lambda bi, ti: (0, 0)),
            pl.BlockSpec((1, CONV_CH), lambda bi, ti: (0, 0)),
            pl.BlockSpec((1, CONV_CH), lambda bi, ti: (0, 0)),
            pl.BlockSpec((1, CONV_CH), lambda bi, ti: (0, 0)),
            pl.BlockSpec((CONV_CH, CONV_CH), lambda bi, ti: (0, 0)),
        ],
        out_specs=pl.BlockSpec((1, CONV_TT, CONV_CH), lambda bi, ti: (bi, ti, 0)),
        scratch_shapes=[pltpu.VMEM((CONV_TT + CONV_HALO, CONV_CH), F32),
                        pltpu.VMEM((CONV_TT, CONV_CH), F32)],
        compiler_params=_cparams(("arbitrary", "arbitrary")),
        name="conformer_conv",
    )(pa, pa, w_dw, b_dw.reshape(1, -1), ln_g.reshape(1, -1), ln_b.reshape(1, -1), w_pw.astype(BF16))


def _compress_kernel(x_ref, pe_ref, w1_ref, b1_ref, w2_ref, kg_ref, o_ref, acc_ref, *, nj, normalise, transpose_out):
    acc_ref[...] = jnp.zeros_like(acc_ref)
    for c in range(CMP_PER_SEL):
        for pp in range(CMP_STRIDE):
            zc = x_ref[0, pl.ds(CMP_STRIDE * c + pp, nj, stride=SEL_BLK), :]
            lo = (zc + pe_ref[0, pp:pp + 1, :]).astype(BF16)
            hi = (zc + pe_ref[0, CMP_STRIDE + pp:CMP_STRIDE + pp + 1, :]).astype(BF16)
            acc_ref[c] += jnp.dot(lo, w1_ref[0, pp], preferred_element_type=F32)
            acc_ref[CMP_PER_SEL + (c - 1) % CMP_PER_SEL] += jnp.dot(hi, w1_ref[0, CMP_STRIDE + pp],
                                                                  preferred_element_type=F32)
    for r in range(CMP_PER_SEL):
        second = acc_ref[CMP_PER_SEL + r]
        if r == CMP_PER_SEL - 1:
            second = pltpu.roll(second, nj - 1, 0)
        act = jax.nn.gelu(acc_ref[r] + second + b1_ref[0])
        if transpose_out:
            out = jnp.dot(w2_ref[0], act.T.astype(BF16), preferred_element_type=F32)
            o_ref[0, 0, :, r * nj:(r + 1) * nj] = out.astype(o_ref.dtype)
        else:
            out = jnp.dot(act.astype(BF16), w2_ref[0], preferred_element_type=F32)
            if normalise:
                ms = jnp.mean(out * out, axis=-1, keepdims=True)
                out = out * lax.rsqrt(ms + NORM_EPS) * kg_ref[...]
            o_ref[0, 0, r * nj:(r + 1) * nj, :] = out.astype(o_ref.dtype)


def _compress(pa, col0, pe, w1, b1, w2, kgain, *, normalise, transpose_out):
    b, t, _ = pa.shape
    nj = t // SEL_BLK
    ncp = CMP_PER_SEL * nj
    cb0 = col0 // HEAD_DIM
    oshape = (b, NSA_KV, HEAD_DIM, ncp) if transpose_out else (b, NSA_KV, ncp, HEAD_DIM)
    oblock = (1, 1, HEAD_DIM, ncp) if transpose_out else (1, 1, ncp, HEAD_DIM)
    assert not (normalise and transpose_out)
    w2_in = (w2.T if transpose_out else w2)[None].astype(BF16)
    return pl.pallas_call(
        functools.partial(_compress_kernel, nj=nj, normalise=normalise, transpose_out=transpose_out),
        out_shape=jax.ShapeDtypeStruct(oshape, BF16),
        grid=(b, NSA_KV),
        in_specs=[
            pl.BlockSpec((1, t, HEAD_DIM), lambda bi, g: (bi, 0, cb0 + g)),
            pl.BlockSpec((1, CMP_BLK, HEAD_DIM), lambda bi, g: (0, 0, 0)),
            pl.BlockSpec((1, CMP_BLK, HEAD_DIM, CMP_HIDDEN), lambda bi, g: (0, 0, 0, 0)),
            pl.BlockSpec((1, 1, CMP_HIDDEN), lambda bi, g: (0, 0, 0)),
            pl.BlockSpec((1,) + w2_in.shape[1:], lambda bi, g: (0, 0, 0)),
            pl.BlockSpec((1, HEAD_DIM), lambda bi, g: (0, 0)),
        ],
        out_specs=pl.BlockSpec(oblock, lambda bi, g: (bi, g, 0, 0)),
        scratch_shapes=[pltpu.VMEM((2 * CMP_PER_SEL, nj, CMP_HIDDEN), F32)],
        compiler_params=_cparams(("arbitrary", "arbitrary")),
        name="nsa_compress_v" if transpose_out else "nsa_compress_k",
    )(pa, pe[None], w1.reshape(1, CMP_BLK, HEAD_DIM, CMP_HIDDEN).astype(BF16), b1.reshape(1, 1, -1),
      w2_in, kgain.reshape(1, -1))


TR_TT = 512


def _transpose_kernel(x_ref, o_ref):
    o_ref[0, 0] = x_ref[0].astype(F32).T.astype(o_ref.dtype)


def _transpose_heads(pb, col_blocks):
    b, t, _ = pb.shape
    outs = []
    for cb in col_blocks:
        outs.append(pl.pallas_call(
            _transpose_kernel,
            out_shape=jax.ShapeDtypeStruct((b, 1, HEAD_DIM, t), BF16),
            grid=(b, t // TR_TT),
            in_specs=[pl.BlockSpec((1, TR_TT, HEAD_DIM), lambda bi, ti, cb=cb: (bi, ti, cb))],
            out_specs=pl.BlockSpec((1, 1, HEAD_DIM, TR_TT), lambda bi, ti: (bi, 0, 0, ti)),
            compiler_params=_cparams(("arbitrary", "arbitrary")),
            name="value_transpose",
        )(pb))
    return outs


def _query_positions(start, nq):
    col = lax.broadcasted_iota(jnp.int32, (1, NSA_HPG * nq), 1)
    return start + col % nq


def _transpose_queries(q):
    qf = q.astype(F32)
    return jnp.concatenate([qf[:, h * HEAD_DIM:(h + 1) * HEAD_DIM].T for h in range(NSA_HPG)], axis=1)


def _store_heads(o_ref, acc, nq):
    for h in range(NSA_HPG):
        o_ref[0, :, h * HEAD_DIM:(h + 1) * HEAD_DIM] = acc[:, h * nq:(h + 1) * nq].T.astype(o_ref.dtype)


def _online_tile(s, vt, m, l, acc_ref):
    mt = jnp.max(s, axis=0, keepdims=True)
    mn = jnp.maximum(m, mt)
    alpha = jnp.exp(m - mn)
    p = jnp.exp(s - mn)
    l = alpha * l + jnp.sum(p, axis=0, keepdims=True)
    acc_ref[...] = alpha * acc_ref[...] + jnp.dot(vt, p.astype(BF16), preferred_element_type=F32)
    return mn, l


def _nsa_cmp_kernel(q_ref, kc_ref, vct_ref, o_ref, qs_ref, *, nq, nj, nselp, n_top):
    start = pl.program_id(2) * nq
    qt = _transpose_queries(q_ref[0]).astype(BF16)
    s = jnp.dot(kc_ref[0, 0], qt, preferred_element_type=F32)
    ncp = CMP_PER_SEL * nj
    row = lax.broadcasted_iota(jnp.int32, (ncp, 1), 0)
    cmp_end = SEL_BLK * (row % nj) + CMP_STRIDE * (row // nj) + (CMP_BLK - 1)
    tpos = _query_positions(start, nq)
    s = jnp.where(cmp_end <= tpos, s, NEG)
    m = jnp.max(s, axis=0, keepdims=True)
    m = jnp.where(m <= 0.5 * NEG, 0.0, m)
    p = jnp.exp(s - m)
    den = jnp.maximum(jnp.sum(p, axis=0, keepdims=True), 1e-30)
    p = p / den
    _store_heads(o_ref, jnp.dot(vct_ref[0, 0], p.astype(BF16), preferred_element_type=F32), nq)

    imp = p[:, 0:nq] + p[:, nq:2 * nq] + p[:, 2 * nq:3 * nq]
    last = imp[(CMP_PER_SEL - 1) * nj:, :]
    jrow = lax.broadcasted_iota(jnp.int32, (nj, 1), 0)
    prev = jnp.where(jrow == 0, 0.0, pltpu.roll(last, 1, 0))
    imp_sel = prev + last
    for r in range(CMP_PER_SEL - 1):
        imp_sel = imp_sel + imp[r * nj:(r + 1) * nj, :]

    tq = tpos[:, :nq]
    cur = tq // SEL_BLK
    forced = (jrow == 0) | (jrow == cur) | (jrow == cur - 1)
    score = jnp.where(forced, -NEG, jnp.where(jrow <= cur, imp_sel, NEG))
    sel = jnp.zeros((nj, nq), F32)
    for _ in range(n_top):
        mx = jnp.max(score, axis=0, keepdims=True)
        idx = jnp.min(jnp.where(score == mx, jrow, nj), axis=0, keepdims=True)
        hit = jrow == idx
        sel = jnp.where(hit & (mx > 0.5 * NEG), 1.0, sel)
        score = jnp.where(hit, 2.0 * NEG, score)
    selneg = jnp.where(sel > 0.0, 0.0, NEG)
    if nselp > nj:
        selneg = jnp.concatenate([selneg, jnp.full((nselp - nj, nq), NEG, F32)], axis=0)
    selneg = jnp.concatenate([selneg] * NSA_HPG, axis=1).astype(BF16)
    for hf in range(nselp // LANE):
        qs_ref[0, 0, 0, 2 * LANE * hf:2 * LANE * hf + LANE, :] = qt
        qs_ref[0, 0, 0, 2 * LANE * hf + LANE:2 * LANE * (hf + 1), :] = selneg[hf * LANE:(hf + 1) * LANE, :]


def _nsa_cmp(pb, kc, vct, t):
    b = pb.shape[0]
    nq = QBLK
    nqb = t // nq
    nj = t // SEL_BLK
    nselp = -(-nj // LANE) * LANE
    n_top = min(SEL_TOP, nj)
    ncp = CMP_PER_SEL * nj
    return pl.pallas_call(
        functools.partial(_nsa_cmp_kernel, nq=nq, nj=nj, nselp=nselp, n_top=n_top),
        out_shape=(jax.ShapeDtypeStruct((b, t, NSA_HEADS * HEAD_DIM), F32),
                   jax.ShapeDtypeStruct((b, NSA_KV, nqb, 2 * nselp, NSA_HPG * nq), BF16)),
        grid=(b, NSA_KV, nqb),
        in_specs=[
            pl.BlockSpec((1, nq, NSA_HPG * HEAD_DIM), lambda bi, g, qi: (bi, qi, g)),
            pl.BlockSpec((1, 1, ncp, HEAD_DIM), lambda bi, g, qi: (bi, g, 0, 0)),
            pl.BlockSpec((1, 1, HEAD_DIM, ncp), lambda bi, g, qi: (bi, g, 0, 0)),
        ],
        out_specs=(pl.BlockSpec((1, nq, NSA_HPG * HEAD_DIM), lambda bi, g, qi: (bi, qi, g)),
                   pl.BlockSpec((1, 1, 1, 2 * nselp, NSA_HPG * nq), lambda bi, g, qi: (bi, g, qi, 0, 0))),
        compiler_params=_cparams(("arbitrary", "arbitrary", "arbitrary")),
        name="nsa_compressed_topk",
    )(pb, kc, vct)


def _nsa_slc_kernel(qs_ref, k_ref, vt_ref, o_ref, acc_ref, *, nq):
    qi = pl.program_id(2)
    start = qi * nq
    ncol = NSA_HPG * nq
    tiles_per_half = LANE * SEL_BLK // KVT
    blocks_per_tile = KVT // SEL_BLK
    krow = lax.broadcasted_iota(jnp.int32, (KVT, 1), 0)
    lane = lax.broadcasted_iota(jnp.int32, (KVT, LANE), 1)
    acc_ref[...] = jnp.zeros_like(acc_ref)

    def tile(ti, carry, diagonal):
        m, l = carry
        k0 = pl.multiple_of(ti * KVT, KVT)
        half = ti // tiles_per_half
        onehot = (lane == (ti % tiles_per_half) * blocks_per_tile + krow // SEL_BLK).astype(BF16)
        ka = jnp.concatenate([k_ref[0, pl.ds(k0, KVT), :], onehot], axis=1)
        qa = qs_ref[0, 0, 0, pl.ds(pl.multiple_of(half * 2 * LANE, 2 * LANE), 2 * LANE), :]
        s = jnp.dot(ka, qa, preferred_element_type=F32)
        if diagonal:
            s = jnp.where(k0 + krow <= _query_positions(start, nq), s, NEG)
        return _online_tile(s, vt_ref[0, 0, :, pl.ds(k0, KVT)], m, l, acc_ref)

    init = (jnp.full((1, ncol), NEG, F32), jnp.zeros((1, ncol), F32))
    carry = lax.fori_loop(0, qi, lambda ti, c: tile(ti, c, False), init)
    _, l = tile(qi, carry, True)
    _store_heads(o_ref, acc_ref[...] / l, nq)


def _nsa_slc(qsel, pb, k_cb, vt, t):
    b = pb.shape[0]
    nq = QBLK
    nqb = t // nq
    rows = qsel.shape[3]
    return pl.pallas_call(
        functools.partial(_nsa_slc_kernel, nq=nq),
        out_shape=jax.ShapeDtypeStruct((b, t, NSA_HEADS * HEAD_DIM), F32),
        grid=(b, NSA_KV, nqb),
        in_specs=[
            pl.BlockSpec((1, 1, 1, rows, NSA_HPG * nq), lambda bi, g, qi: (bi, g, qi, 0, 0)),
            pl.BlockSpec((1, t, HEAD_DIM), lambda bi, g, qi: (bi, 0, k_cb + g)),
            pl.BlockSpec((1, 1, HEAD_DIM, t), lambda bi, g, qi: (bi, g, 0, 0)),
        ],
        out_specs=pl.BlockSpec((1, nq, NSA_HPG * HEAD_DIM), lambda bi, g, qi: (bi, qi, g)),
        scratch_shapes=[pltpu.VMEM((HEAD_DIM, NSA_HPG * nq), F32)],
        compiler_params=_cparams(("arbitrary", "arbitrary", "arbitrary")),
        name="nsa_selected",
    )(qsel, pb, vt)


def _band_kernel(q_ref, k_ref, vt_ref, sink_ref, o_ref, acc_ref, *, nq, window, q_transposed, use_sink):
    qi = pl.program_id(2)
    start = qi * nq
    ncol = NSA_HPG * nq
    if q_transposed:
        qt = q_ref[0, 0, 0]
    else:
        qt = _transpose_queries(q_ref[0]).astype(BF16)
    tpos = _query_positions(start, nq)
    krow = lax.broadcasted_iota(jnp.int32, (KVT, 1), 0)
    acc_ref[...] = jnp.zeros_like(acc_ref)

    def tile(ti, carry):
        m, l = carry
        k0 = pl.multiple_of(ti * KVT, KVT)
        s = jnp.dot(k_ref[0, pl.ds(k0, KVT), :], qt, preferred_element_type=F32)
        rel = tpos - (k0 + krow)
        s = jnp.where((rel >= 0) & (rel < window), s, NEG)
        return _online_tile(s, vt_ref[0, 0, :, pl.ds(k0, KVT)], m, l, acc_ref)

    if use_sink:
        init = (sink_ref[0], jnp.ones((1, ncol), F32))
    else:
        init = (jnp.full((1, ncol), NEG, F32), jnp.zeros((1, ncol), F32))
    n_back = -(-(window - 1) // KVT)
    _, l = lax.fori_loop(jnp.maximum(qi - n_back, 0), qi + 1, tile, init)
    _store_heads(o_ref, acc_ref[...] / l, nq)


def _band_attention(q_arr, q_spec, pb, k_cb, vt, sink_rows, t, *, window, q_transposed, use_sink, name):
    b = pb.shape[0]
    nq = QBLK
    nqb = t // nq
    return pl.pallas_call(
        functools.partial(_band_kernel, nq=nq, window=window, q_transposed=q_transposed, use_sink=use_sink),
        out_shape=jax.ShapeDtypeStruct((b, t, NSA_HEADS * HEAD_DIM), F32),
        grid=(b, NSA_KV, nqb),
        in_specs=[
            q_spec,
            pl.BlockSpec((1, t, HEAD_DIM), lambda bi, g, qi: (bi, 0, k_cb + g)),
            pl.BlockSpec((1, 1, HEAD_DIM, t), lambda bi, g, qi: (bi, g, 0, 0)),
            pl.BlockSpec((1, 1, NSA_HPG * nq), lambda bi, g, qi: (g, 0, 0)),
        ],
        out_specs=pl.BlockSpec((1, nq, NSA_HPG * HEAD_DIM), lambda bi, g, qi: (bi, qi, g)),
        scratch_shapes=[pltpu.VMEM((HEAD_DIM, NSA_HPG * nq), F32)],
        compiler_params=_cparams(("arbitrary", "arbitrary", "arbitrary")),
        name=name,
    )(q_arr, pb, vt, sink_rows)


def _outproj_kernel(ya_ref, oc_ref, os_ref, ow_ref, gt_ref, yc_ref, gain_ref, w_ref, x_ref, gm_ref, o_ref, y_ref):
    @pl.when(pl.program_id(1) == 0)
    def _():
        gain = gain_ref[...]
        gates = jax.nn.sigmoid(gt_ref[...])

        def put(grp, y):
            cs = slice(grp * HEAD_DIM, (grp + 1) * HEAD_DIM)
            ms = jnp.mean(y * y, axis=-1, keepdims=True)
            y_ref[:, cs] = (y * lax.rsqrt(ms + NORM_EPS) * gain[:, cs]).astype(BF16)

        n_a = CONV_CH // HEAD_DIM
        for i in range(n_a):
            put(i, ya_ref[:, i * HEAD_DIM:(i + 1) * HEAD_DIM])
        for h in range(NSA_HEADS):
            cs = slice(h * HEAD_DIM, (h + 1) * HEAD_DIM)
            y = (gates[:, 3 * h:3 * h + 1] * oc_ref[:, cs] + gates[:, 3 * h + 1:3 * h + 2] * os_ref[:, cs]
                 + gates[:, 3 * h + 2:3 * h + 3] * ow_ref[:, cs])
            put(n_a + h, y)
        for h in range(SWA_HEADS):
            put(n_a + NSA_HEADS + h, yc_ref[:, h * HEAD_DIM:(h + 1) * HEAD_DIM])

    y = jnp.dot(y_ref[...], w_ref[...], preferred_element_type=F32)
    o_ref[...] = x_ref[...] + gm_ref[0] * y


def _outproj(ya, oc, osl, ow, pa, gate_cb, yc, gain, w_out, x2, gm, *, rows_per_batch):
    n, d = x2.shape
    mix = w_out.shape[0]
    tm, tn = 512, 512
    bpb = rows_per_batch // tm
    aw = NSA_HEADS * HEAD_DIM
    return pl.pallas_call(
        _outproj_kernel,
        out_shape=jax.ShapeDtypeStruct((n, d), F32),
        grid=(n // tm, d // tn),
        in_specs=[
            pl.BlockSpec((tm, CONV_CH), lambda i, j: (i, 0)),
            pl.BlockSpec((tm, aw), lambda i, j: (i, 0)),
            pl.BlockSpec((tm, aw), lambda i, j: (i, 0)),
            pl.BlockSpec((tm, aw), lambda i, j: (i, 0)),
            pl.BlockSpec((tm, HEAD_DIM), lambda i, j: (i, gate_cb)),
            pl.BlockSpec((tm, aw), lambda i, j: (i, 0)),
            pl.BlockSpec((1, mix), lambda i, j: (0, 0)),
            pl.BlockSpec((mix, tn), lambda i, j: (0, j)),
            pl.BlockSpec((tm, tn), lambda i, j: (i, j)),
            pl.BlockSpec((1, 1, tn), lambda i, j: (i // bpb, 0, j)),
        ],
        out_specs=pl.BlockSpec((tm, tn), lambda i, j: (i, j)),
        scratch_shapes=[pltpu.VMEM((tm, mix), BF16)],
        compiler_params=_cparams(("arbitrary", "arbitrary")),
        name="mix_outproj",
    )(ya, oc, osl, ow, pa, yc, gain, w_out, x2, gm)


def _ffn_kernel(x_ref, g_ref, sc_ref, sh_ref, gf_ref, wg_ref, wu_ref, wd_ref, o_ref, h_ref):
    f = pl.program_id(1)

    @pl.when(f == 0)
    def _():
        h_ref[...] = _mod_rmsnorm(x_ref[...], g_ref[...], sc_ref[0], sh_ref[0]).astype(BF16)
        o_ref[...] = jnp.zeros_like(o_ref)

    h = h_ref[...]
    a = jnp.dot(h, wg_ref[...], preferred_element_type=F32)
    u = jnp.dot(h, wu_ref[...], preferred_element_type=F32)
    act = (a * jax.nn.sigmoid(a) * u).astype(BF16)
    o_ref[...] += jnp.dot(act, wd_ref[...], preferred_element_type=F32)

    @pl.when(f == pl.num_programs(1) - 1)
    def _():
        o_ref[...] = x_ref[...] + gf_ref[0] * o_ref[...]


def _ffn_dense(x2, g, sc, sh, gf, wg, wu, wd, *, rows_per_batch):
    n, d = x2.shape
    ff = wg.shape[1]
    tm, tf = 512, 512
    bpb = rows_per_batch // tm
    mod_spec = pl.BlockSpec((1, 1, d), lambda i, f: (i // bpb, 0, 0))
    return pl.pallas_call(
        _ffn_kernel,
        out_shape=jax.ShapeDtypeStruct((n, d), F32),
        grid=(n // tm, ff // tf),
        in_specs=[
            pl.BlockSpec((tm, d), lambda i, f: (i, 0)),
            pl.BlockSpec((1, d), lambda i, f: (0, 0)),
            mod_spec, mod_spec, mod_spec,
            pl.BlockSpec((d, tf), lambda i, f: (0, f)),
            pl.BlockSpec((d, tf), lambda i, f: (0, f)),
            pl.BlockSpec((tf, d), lambda i, f: (f, 0)),
        ],
        out_specs=pl.BlockSpec((tm, d), lambda i, f: (i, 0)),
        scratch_shapes=[pltpu.VMEM((tm, d), BF16)],
        compiler_params=_cparams(("arbitrary", "arbitrary")),
        name="ffn_dense",
    )(x2, g, sc, sh, gf, wg, wu, wd)


def _router_kernel(x_ref, g_ref, sc_ref, sh_ref, wr_ref, h_ref, idx_ref, wt_ref):
    h = _mod_rmsnorm(x_ref[...], g_ref[...], sc_ref[0], sh_ref[0])
    h_ref[...] = h
    logits = jnp.dot(h, wr_ref[...], precision=HIGHEST, preferred_element_type=F32)
    e = lax.broadcasted_iota(jnp.int32, logits.shape, 1)
    m1 = jnp.max(logits, axis=-1, keepdims=True)
    i1 = jnp.min(jnp.where(logits == m1, e, N_EXPERTS), axis=-1, keepdims=True)
    rest = jnp.where(e == i1, -jnp.inf, logits)
    m2 = jnp.max(rest, axis=-1, keepdims=True)
    i2 = jnp.min(jnp.where(rest == m2, e, N_EXPERTS), axis=-1, keepdims=True)
    e2 = jnp.exp(m2 - m1)
    den = 1.0 + e2
    idx_ref[:, 0:1] = i1
    idx_ref[:, 1:2] = i2
    wt_ref[:, 0:1] = 1.0 / den
    wt_ref[:, 1:2] = e2 / den


def _router(x2, g, sc, sh, w_router, *, rows_per_batch):
    n, d = x2.shape
    tm = 512
    bpb = rows_per_batch // tm
    mod_spec = pl.BlockSpec((1, 1, d), lambda i: (i // bpb, 0, 0))
    return pl.pallas_call(
        _router_kernel,
        out_shape=(jax.ShapeDtypeStruct((n, d), F32),
                   jax.ShapeDtypeStruct((n, TOP_K), jnp.int32),
                   jax.ShapeDtypeStruct((n, TOP_K), F32)),
        grid=(n // tm,),
        in_specs=[
            pl.BlockSpec((tm, d), lambda i: (i, 0)),
            pl.BlockSpec((1, d), lambda i: (0, 0)),
            mod_spec, mod_spec,
            pl.BlockSpec((d, N_EXPERTS), lambda i: (0, 0)),
        ],
        out_specs=(pl.BlockSpec((tm, d), lambda i: (i, 0)),
                   pl.BlockSpec((tm, TOP_K), lambda i: (i, 0)),
                   pl.BlockSpec((tm, TOP_K), lambda i: (i, 0))),
        compiler_params=_cparams(("arbitrary",)),
        name="moe_router",
    )(x2, g, sc, sh, w_router)


def _expert_kernel(be_ref, src_ref, nused_ref, h_hbm, wg_ref, wu_ref, wd_ref, o_ref, xb_ref, sem):
    i = pl.program_id(0)
    f = pl.program_id(1)
    used = i < nused_ref[0]

    def row_copy(r):
        return pltpu.make_async_copy(h_hbm.at[pl.ds(src_ref[i * MOE_TM + r], 1), :],
                                     xb_ref.at[pl.ds(r, 1), :], sem)

    @pl.when(f == 0)
    def _():
        o_ref[...] = jnp.zeros_like(o_ref)

    @pl.when((f == 0) & used)
    def _():
        lax.fori_loop(0, MOE_TM, lambda r, c: (row_copy(r).start(), c)[1], 0)
        lax.fori_loop(0, MOE_TM, lambda r, c: (row_copy(r).wait(), c)[1], 0)

    @pl.when(used)
    def _():
        xb = xb_ref[...]
        a = jnp.dot(xb, wg_ref[0], preferred_element_type=F32)
        u = jnp.dot(xb, wu_ref[0], preferred_element_type=F32)
        act = (a * jax.nn.sigmoid(a) * u).astype(BF16)
        o_ref[...] += jnp.dot(act, wd_ref[0], preferred_element_type=F32)


def _experts(h_tok, block_expert, src_row, n_used, wg, wu, wd, n_blocks):
    n, d = h_tok.shape
    ff = wg.shape[2]
    tf = 512
    return pl.pallas_call(
        _expert_kernel,
        out_shape=jax.ShapeDtypeStruct((n_blocks * MOE_TM, d), F32),
        grid_spec=pltpu.PrefetchScalarGridSpec(
            num_scalar_prefetch=3,
            grid=(n_blocks, ff // tf),
            in_specs=[
                pl.BlockSpec(memory_space=pl.ANY),
                pl.BlockSpec((1, d, tf), lambda i, f, be, src, nu: (be[i], 0, f)),
                pl.BlockSpec((1, d, tf), lambda i, f, be, src, nu: (be[i], 0, f)),
                pl.BlockSpec((1, tf, d), lambda i, f, be, src, nu: (be[i], f, 0)),
            ],
            out_specs=pl.BlockSpec((MOE_TM, d), lambda i, f, be, src, nu: (i, 0)),
            scratch_shapes=[pltpu.VMEM((MOE_TM, d), F32), pltpu.SemaphoreType.DMA],
        ),
        compiler_params=_cparams(("arbitrary", "arbitrary")),
        name="moe_experts",
    )(block_expert, src_row, n_used, h_tok, wg, wu, wd)


CMB_TM = 256


def _combine_kernel(pos_ref, y_hbm, x_ref, wt_ref, gf_ref, o_ref, yb_ref, sem):
    i = pl.program_id(0)

    def row_copy(r, k):
        return pltpu.make_async_copy(y_hbm.at[pl.ds(pos_ref[(i * CMB_TM + r) * TOP_K + k], 1), :],
                                     yb_ref.at[k, pl.ds(r, 1), :], sem)

    def start(r, c):
        for k in range(TOP_K):
            row_copy(r, k).start()
        return c

    def wait(r, c):
        for k in range(TOP_K):
            row_copy(r, k).wait()
        return c

    lax.fori_loop(0, CMB_TM, start, 0)
    lax.fori_loop(0, CMB_TM, wait, 0)
    wt = wt_ref[...]
    f = yb_ref[0] * wt[:, 0:1]
    for k in range(1, TOP_K):
        f = f + yb_ref[k] * wt[:, k:k + 1]
    o_ref[...] = x_ref[...] + gf_ref[0] * f


def _combine(pos, y_buf, x2, wt, gf, *, rows_per_batch):
    n, d = x2.shape
    bpb = rows_per_batch // CMB_TM
    return pl.pallas_call(
        _combine_kernel,
        out_shape=jax.ShapeDtypeStruct((n, d), F32),
        grid_spec=pltpu.PrefetchScalarGridSpec(
            num_scalar_prefetch=1,
            grid=(n // CMB_TM,),
            in_specs=[
                pl.BlockSpec(memory_space=pl.ANY),
                pl.BlockSpec((CMB_TM, d), lambda i, pos: (i, 0)),
                pl.BlockSpec((CMB_TM, TOP_K), lambda i, pos: (i, 0)),
                pl.BlockSpec((1, 1, d), lambda i, pos: (i // bpb, 0, 0)),
            ],
            out_specs=pl.BlockSpec((CMB_TM, d), lambda i, pos: (i, 0)),
            scratch_shapes=[pltpu.VMEM((TOP_K, CMB_TM, d), F32), pltpu.SemaphoreType.DMA],
        ),
        compiler_params=_cparams(("arbitrary",)),
        name="moe_combine",
    )(pos, y_buf, x2, wt, gf)


def _moe(x2, g, sc, sh, gf, w_router, wg, wu, wd, *, rows_per_batch):
    n, d = x2.shape
    h_tok, top_idx, top_w = _router(x2, g, sc, sh, w_router, rows_per_batch=rows_per_batch)
    n_assign = n * TOP_K
    n_blocks = n_assign // MOE_TM + N_EXPERTS
    expert_of = top_idx.reshape(-1)
    onehot = (expert_of[:, None] == jnp.arange(N_EXPERTS)[None, :]).astype(jnp.int32)
    rank = jnp.take_along_axis(jnp.cumsum(onehot, axis=0) - onehot, expert_of[:, None], axis=1)[:, 0]
    counts = jnp.sum(onehot, axis=0)
    padded = (counts + MOE_TM - 1) // MOE_TM * MOE_TM
    pad_ends = jnp.cumsum(padded)
    pad_starts = pad_ends - padded
    dest = pad_starts[expert_of] + rank
    token_of = jnp.arange(n_assign, dtype=jnp.int32) // TOP_K
    src_row = jnp.zeros((n_blocks * MOE_TM,), jnp.int32).at[dest].set(token_of)
    block_start = jnp.arange(n_blocks, dtype=jnp.int32) * MOE_TM
    block_expert = jnp.minimum(jnp.sum(block_start[:, None] >= pad_ends[None, :], axis=1),
                               N_EXPERTS - 1).astype(jnp.int32)
    n_used = (pad_ends[-1] // MOE_TM).astype(jnp.int32).reshape(1)
    y_buf = _experts(h_tok, block_expert, src_row, n_used, wg, wu, wd, n_blocks)
    return _combine(dest.astype(jnp.int32), y_buf, x2, top_w, gf, rows_per_batch=rows_per_batch)


PA_KC = 2 * CONV_CH
PA_VC = PA_KC + NSA_KV * HEAD_DIM
PA_GATE = PA_VC + NSA_KV * HEAD_DIM
PA_WIDTH = PA_GATE + LANE
PB_QN = 0
PB_QS = NSA_HEADS * HEAD_DIM
PB_KS = PB_QS + SWA_HEADS * HEAD_DIM
PB_VS = PB_KS + NSA_KV * HEAD_DIM
PB_KW = PB_VS + NSA_KV * HEAD_DIM
PB_VW = PB_KW + NSA_KV * HEAD_DIM
PB_KC = PB_VW + NSA_KV * HEAD_DIM
PB_VC = PB_KC + SWA_KV * HEAD_DIM
PB_WIDTH = PB_VC + SWA_KV * HEAD_DIM


def _split_w_in(w_in, nsa_q_g, nsa_k_g, swa_q_g, swa_k_g):
    sizes = [2 * CONV_CH, NSA_HEADS * HEAD_DIM] + [NSA_KV * HEAD_DIM] * 6 + [
        3 * NSA_HEADS, SWA_HEADS * HEAD_DIM, SWA_KV * HEAD_DIM, SWA_KV * HEAD_DIM]
    offs = [0]
    for s in sizes:
        offs.append(offs[-1] + s)
    seg = [w_in[:, offs[i]:offs[i + 1]] for i in range(len(sizes))]
    (u_conv, q_n, kc, vc, ks, vs, kw, vw, g_n, q_s, k_s, v_s) = seg
    d = w_in.shape[0]
    g_pad = jnp.zeros((d, LANE - 3 * NSA_HEADS), w_in.dtype)
    wa = jnp.concatenate([u_conv, kc, vc, g_n, g_pad], axis=1).astype(BF16)
    wb = jnp.concatenate([q_n, q_s, ks, vs, kw, vw, k_s, v_s], axis=1).astype(BF16)
    scale = HEAD_DIM ** -0.5
    ones = jnp.ones((HEAD_DIM,), F32)

    def rep(v, n):
        return jnp.tile(v.astype(F32), n)

    gain = jnp.concatenate([rep(nsa_q_g * scale, NSA_HEADS), rep(swa_q_g * scale, SWA_HEADS),
                            rep(nsa_k_g[1], NSA_KV), rep(ones, NSA_KV), rep(nsa_k_g[2], NSA_KV),
                            rep(ones, NSA_KV), rep(swa_k_g, SWA_KV), rep(ones, SWA_KV)]).reshape(1, -1)
    one = jnp.ones((HEAD_DIM,), F32)
    zero = jnp.zeros((HEAD_DIM,), F32)
    flag = jnp.concatenate([rep(one, NSA_HEADS + SWA_HEADS), rep(one, NSA_KV), rep(zero, NSA_KV),
                            rep(one, NSA_KV), rep(zero, NSA_KV), rep(one, SWA_KV),
                            rep(zero, SWA_KV)]).reshape(1, -1)
    return wa, wb, gain, flag


def _mixer(x2, b, t, norm_g, sc, sh, gm, w_in, conv_dw_w, conv_dw_b, conv_ln_g, conv_ln_b, conv_pw_w, cmp_pe,
           cmp_w1, cmp_b1, cmp_w2, nsa_q_g, nsa_k_g, swa_q_g, swa_k_g, swa_sinks, grp_out_g, w_out):
    n, d = x2.shape
    wa, wb, gain, flag = _split_w_in(w_in, nsa_q_g, nsa_k_g, swa_q_g, swa_k_g)
    dummy = jnp.zeros((1, PA_WIDTH), F32)
    pa = _inproj(x2, norm_g, sc, sh, wa, dummy, dummy, rows_per_batch=t, tm=512, tn=PA_WIDTH,
                 out_dtype=F32, epilogue=False, name="inproj_f32")
    pb = _inproj(x2, norm_g, sc, sh, wb, gain, flag, rows_per_batch=t, tm=1024, tn=512,
                 out_dtype=BF16, epilogue=True, name="inproj_bf16")
    pa3 = pa.reshape(b, t, PA_WIDTH)
    pb3 = pb.reshape(b, t, PB_WIDTH)

    y_a = _conformer_conv(pa3, conv_dw_w, conv_dw_b, conv_ln_g, conv_ln_b, conv_pw_w)

    k_cmp = _compress(pa3, PA_KC, cmp_pe[0], cmp_w1[0], cmp_b1[0], cmp_w2[0], nsa_k_g[0],
                      normalise=True, transpose_out=False)
    v_cmp_t = _compress(pa3, PA_VC, cmp_pe[1], cmp_w1[1], cmp_b1[1], cmp_w2[1], nsa_k_g[0],
                        normalise=False, transpose_out=True)
    vts = _transpose_heads(pb3, [(base // HEAD_DIM) + g for base in (PB_VS, PB_VW, PB_VC)
                                 for g in range(NSA_KV)])
    vt_slc = jnp.concatenate(vts[0:2], axis=1)
    vt_win = jnp.concatenate(vts[2:4], axis=1)
    vt_swa = jnp.concatenate(vts[4:6], axis=1)
    o_cmp, qsel = _nsa_cmp(pb3, k_cmp, v_cmp_t, t)
    o_slc = _nsa_slc(qsel, pb3, PB_KS // HEAD_DIM, vt_slc, t)
    no_sink = jnp.zeros((NSA_KV, 1, NSA_HPG * QBLK), F32)
    qt_spec = pl.BlockSpec((1, 1, 1, HEAD_DIM, NSA_HPG * QBLK), lambda bi, g, qi: (bi, g, qi, 0, 0))
    o_win = _band_attention(qsel, qt_spec, pb3, PB_KW // HEAD_DIM, vt_win, no_sink, t, window=NSA_WIN,
                            q_transposed=True, use_sink=False, name="nsa_window")

    sink_rows = jnp.repeat(swa_sinks.astype(F32).reshape(SWA_KV, SWA_HPG), QBLK, axis=1).reshape(
        SWA_KV, 1, SWA_HPG * QBLK)
    qs_cb = PB_QS // (SWA_HPG * HEAD_DIM)
    q_spec = pl.BlockSpec((1, QBLK, SWA_HPG * HEAD_DIM), lambda bi, g, qi: (bi, qi, qs_cb + g))
    y_c = _band_attention(pb3, q_spec, pb3, PB_KC // HEAD_DIM, vt_swa, sink_rows, t, window=SWA_WIN,
                          q_transposed=False, use_sink=True, name="swa_sink")

    aw = NSA_HEADS * HEAD_DIM
    return _outproj(y_a.reshape(n, CONV_CH), o_cmp.reshape(n, aw), o_slc.reshape(n, aw), o_win.reshape(n, aw),
                    pa, PA_GATE // HEAD_DIM, y_c.reshape(n, aw), grp_out_g.reshape(1, -1), w_out.astype(BF16),
                    x2, gm, rows_per_batch=t)


def kernel(x, c, norm_mix_g, norm_ffn_g, w_ada, b_ada, w_in, conv_dw_w, conv_dw_b, conv_ln_g, conv_ln_b, conv_pw_w, cmp_pe, cmp_w1, cmp_b1, cmp_w2, nsa_q_g, nsa_k_g, swa_q_g, swa_k_g, swa_sinks, grp_out_g, w_out, ffn_w_gate, ffn_w_up, ffn_w_down, moe_router, moe_w_gate, moe_w_up, moe_w_down):
    b, t, d = x.shape
    depth = w_in.shape[0]
    mod = _adaln(c, w_ada, b_ada)
    x2 = x.reshape(b * t, d)
    for i in range(depth):
        sh_m, sc_m, g_m, sh_f, sc_f, g_f = [mod[i, :, k, :].reshape(b, 1, d) for k in range(N_ADA)]
        x2 = _mixer(x2, b, t, norm_mix_g[i].reshape(1, d), sc_m, sh_m, g_m, w_in[i], conv_dw_w[i], conv_dw_b[i],
                    conv_ln_g[i], conv_ln_b[i], conv_pw_w[i], cmp_pe[i], cmp_w1[i], cmp_b1[i], cmp_w2[i],
                    nsa_q_g[i], nsa_k_g[i], swa_q_g[i], swa_k_g[i], swa_sinks[i], grp_out_g[i], w_out[i])
        j = i // 2
        gf = norm_ffn_g[i].reshape(1, d)
        if i % 2 == 0:
            x2 = _ffn_dense(x2, gf, sc_f, sh_f, g_f, ffn_w_gate[j].astype(BF16), ffn_w_up[j].astype(BF16),
                            ffn_w_down[j].astype(BF16), rows_per_batch=t)
        else:
            x2 = _moe(x2, gf, sc_f, sh_f, g_f, moe_router[j], moe_w_gate[j].astype(BF16),
                      moe_w_up[j].astype(BF16), moe_w_down[j].astype(BF16), rows_per_batch=t)
    return x2.reshape(b, t, d)
```

```python
import functools
import math

import jax
import jax.numpy as jnp
from jax import lax
from jax.experimental import pallas as pl
from jax.experimental.pallas import tpu as pltpu

F32 = jnp.float32
BF16 = jnp.bfloat16
HIGHEST = lax.Precision.HIGHEST

HEAD_DIM = 128
CONV_CH = 512
CONV_K = 31
NSA_HEADS = 6
NSA_KV = 2
NSA_HPG = NSA_HEADS // NSA_KV
CMP_BLK = 32
CMP_STRIDE = 16
CMP_HIDDEN = 256
SEL_BLK = 64
SEL_TOP = 16
CMP_PER_SEL = SEL_BLK // CMP_STRIDE
NSA_WIN = 512
SWA_HEADS = 6
SWA_KV = 2
SWA_HPG = SWA_HEADS // SWA_KV
SWA_WIN = 128
N_EXPERTS = 8
TOP_K = 2
N_ADA = 6
NORM_EPS = 1e-6
LN_EPS = 1e-5
LOG2E = math.log2(math.e)

LANE = 128
SUBLANE = 8
BF16_ROWS = 16
VMEM_LIMIT = 56 * 1024 * 1024

QBLK = 256
KVT = 256
VROWS = HEAD_DIM + BF16_ROWS
MOE_TM = 1024
NEG = -1e30


def _cparams(sem):
    return pltpu.CompilerParams(dimension_semantics=sem, vmem_limit_bytes=VMEM_LIMIT)


def _mod_rmsnorm(x, g, sc, sh):
    ms = jnp.mean(x * x, axis=-1, keepdims=True)
    y = x * lax.rsqrt(ms + NORM_EPS) * g
    return y * (1.0 + sc) + sh


def _adaln_kernel(c_ref, w_ref, b_ref, o_ref):
    c = c_ref[...]
    ca = c * jax.nn.sigmoid(c)
    o_ref[0] = jnp.dot(ca, w_ref[0], precision=HIGHEST, preferred_element_type=F32) + b_ref[0]


def _adaln(c, w_ada, b_ada):
    n_layers, d, width = w_ada.shape
    b = c.shape[0]
    rows = -(-b // SUBLANE) * SUBLANE
    c_pad = jnp.zeros((rows, d), F32).at[:b].set(c)
    tn = 1024
    out = pl.pallas_call(
        _adaln_kernel,
        out_shape=jax.ShapeDtypeStruct((n_layers, rows, width), F32),
        grid=(n_layers, width // tn),
        in_specs=[
            pl.BlockSpec((rows, d), lambda l, j: (0, 0)),
            pl.BlockSpec((1, d, tn), lambda l, j: (l, 0, j)),
            pl.BlockSpec((1, 1, tn), lambda l, j: (l, 0, j)),
        ],
        out_specs=pl.BlockSpec((1, rows, tn), lambda l, j: (l, 0, j)),
        compiler_params=_cparams(("arbitrary", "arbitrary")),
        name="adaln",
    )(c_pad, w_ada, b_ada.reshape(n_layers, 1, width))
    return out[:, :b].reshape(n_layers, b, N_ADA, d)


def _inproj_kernel(x_ref, g_ref, sc_ref, sh_ref, w_ref, gain_ref, flag_ref, o_ref, h_ref, *, epilogue):
    @pl.when(pl.program_id(1) == 0)
    def _():
        h_ref[...] = _mod_rmsnorm(x_ref[...], g_ref[...], sc_ref[0], sh_ref[0]).astype(BF16)

    y = jnp.dot(h_ref[...], w_ref[...], preferred_element_type=F32)
    if epilogue:
        gain = gain_ref[...]
        flag = flag_ref[...]
        for hh in range(y.shape[1] // HEAD_DIM):
            cs = slice(hh * HEAD_DIM, (hh + 1) * HEAD_DIM)
            yh = y[:, cs]
            ms = jnp.mean(yh * yh, axis=-1, keepdims=True)
            yn = yh * lax.rsqrt(ms + NORM_EPS) * gain[:, cs]
            o_ref[:, cs] = jnp.where(flag[:, cs] > 0.0, yn, yh).astype(o_ref.dtype)
    else:
        o_ref[...] = y.astype(o_ref.dtype)


def _inproj(x2, g, sc, sh, w, gain, flag, *, rows_per_batch, tm, tn, out_dtype, epilogue, name):
    n, d = x2.shape
    width = w.shape[1]
    bpb = rows_per_batch // tm
    return pl.pallas_call(
        functools.partial(_inproj_kernel, epilogue=epilogue),
        out_shape=jax.ShapeDtypeStruct((n, width), out_dtype),
        grid=(n // tm, width // tn),
        in_specs=[
            pl.BlockSpec((tm, d), lambda i, j: (i, 0)),
            pl.BlockSpec((1, d), lambda i, j: (0, 0)),
            pl.BlockSpec((1, 1, d), lambda i, j: (i // bpb, 0, 0)),
            pl.BlockSpec((1, 1, d), lambda i, j: (i // bpb, 0, 0)),
            pl.BlockSpec((d, tn), lambda i, j: (0, j)),
            pl.BlockSpec((1, tn), lambda i, j: (0, j)),
            pl.BlockSpec((1, tn), lambda i, j: (0, j)),
        ],
        out_specs=pl.BlockSpec((tm, tn), lambda i, j: (i, j)),
        scratch_shapes=[pltpu.VMEM((tm, d), BF16)],
        compiler_params=_cparams(("arbitrary", "arbitrary")),
        name=name,
    )(x2, g, sc, sh, w, gain, flag)


CONV_TT = 256
CONV_HALO = 32
CONV_CHUNK = 32


def _conv_kernel(u_ref, halo_ref, wdw_ref, bdw_ref, lng_ref, lnb_ref, wpw_ref, og_ref, o_ref, z_ref, c_ref):
    t = pl.program_id(1)
    u = u_ref[0]
    z_ref[CONV_HALO:, :] = u[:, :CONV_CH] * jax.nn.sigmoid(u[:, CONV_CH:])
    hu = halo_ref[0]
    zh = hu[:, :CONV_CH] * jax.nn.sigmoid(hu[:, CONV_CH:])
    z_ref[:CONV_HALO, :] = jnp.where(t == 0, 0.0, zh)
    wdw = wdw_ref[...]
    off = CONV_HALO - (CONV_K - 1)
    for c in range(CONV_TT // CONV_CHUNK):
        acc = jnp.zeros((CONV_CHUNK, CONV_CH), F32) + bdw_ref[...]
        for k in range(CONV_K):
            lo = c * CONV_CHUNK + off + k
            acc = acc + wdw[k:k + 1, :] * z_ref[lo:lo + CONV_CHUNK, :]
        c_ref[c * CONV_CHUNK:(c + 1) * CONV_CHUNK, :] = acc
    y = c_ref[...]
    mu = jnp.mean(y, axis=-1, keepdims=True)
    var = jnp.mean(jnp.square(y - mu), axis=-1, keepdims=True)
    y = (y - mu) * lax.rsqrt(var + LN_EPS) * lng_ref[...] + lnb_ref[...]
    y = y * jax.nn.sigmoid(y)
    y = jnp.dot(y.astype(BF16), wpw_ref[...], preferred_element_type=F32)
    og = og_ref[...]
    for grp in range(CONV_CH // HEAD_DIM):
        cs = slice(grp * HEAD_DIM, (grp + 1) * HEAD_DIM)
        yh = y[:, cs]
        ms = jnp.mean(yh * yh, axis=-1, keepdims=True)
        o_ref[0, :, cs] = (yh * lax.rsqrt(ms + NORM_EPS) * og[:, cs]).astype(o_ref.dtype)


def _conformer_conv(pa, w_dw, b_dw, ln_g, ln_b, w_pw, out_gain):
    b, t, _ = pa.shape
    hb = CONV_TT // CONV_HALO
    row = lambda v: v.reshape(1, -1)
    vec_spec = pl.BlockSpec((1, CONV_CH), lambda bi, ti: (0, 0))
    return pl.pallas_call(
        _conv_kernel,
        out_shape=jax.ShapeDtypeStruct((b, t, CONV_CH), BF16),
        grid=(b, t // CONV_TT),
        in_specs=[
            pl.BlockSpec((1, CONV_TT, 2 * CONV_CH), lambda bi, ti: (bi, ti, 0)),
            pl.BlockSpec((1, CONV_HALO, 2 * CONV_CH), lambda bi, ti: (bi, jnp.maximum(ti * hb - 1, 0), 0)),
            pl.BlockSpec((CONV_K, CONV_CH), lambda bi, ti: (0, 0)),
            vec_spec, vec_spec, vec_spec,
            pl.BlockSpec((CONV_CH, CONV_CH), lambda bi, ti: (0, 0)),
            vec_spec,
        ],
        out_specs=pl.BlockSpec((1, CONV_TT, CONV_CH), lambda bi, ti: (bi, ti, 0)),
        scratch_shapes=[pltpu.VMEM((CONV_TT + CONV_HALO, CONV_CH), F32),
                        pltpu.VMEM((CONV_TT, CONV_CH), F32)],
        compiler_params=_cparams(("arbitrary", "arbitrary")),
        name="conformer_conv",
    )(pa, pa, w_dw, row(b_dw), row(ln_g), row(ln_b), w_pw.astype(BF16), row(out_gain))


def _compress_kernel(x_ref, pe_ref, w1_ref, b1_ref, w2_ref, kg_ref, o_ref, acc_ref, *, nj, normalise, transpose_out):
    acc_ref[...] = jnp.zeros_like(acc_ref)
    for c in range(CMP_PER_SEL):
        for pp in range(CMP_STRIDE):
            zc = x_ref[0, pl.ds(CMP_STRIDE * c + pp, nj, stride=SEL_BLK), :]
            lo = (zc + pe_ref[0, pp:pp + 1, :]).astype(BF16)
            hi = (zc + pe_ref[0, CMP_STRIDE + pp:CMP_STRIDE + pp + 1, :]).astype(BF16)
            acc_ref[c] += jnp.dot(lo, w1_ref[0, pp], preferred_element_type=F32)
            acc_ref[CMP_PER_SEL + (c - 1) % CMP_PER_SEL] += jnp.dot(hi, w1_ref[0, CMP_STRIDE + pp],
                                                                  preferred_element_type=F32)
    if transpose_out:
        ones_row = lax.broadcasted_iota(jnp.int32, (BF16_ROWS, CMP_PER_SEL * nj), 0) == 0
        o_ref[0, 0, HEAD_DIM:, :] = ones_row.astype(o_ref.dtype)
    for r in range(CMP_PER_SEL):
        second = acc_ref[CMP_PER_SEL + r]
        if r == CMP_PER_SEL - 1:
            second = pltpu.roll(second, nj - 1, 0)
        act = jax.nn.gelu(acc_ref[r] + second + b1_ref[0])
        if transpose_out:
            out = jnp.dot(w2_ref[0], act.T.astype(BF16), preferred_element_type=F32)
            o_ref[0, 0, :HEAD_DIM, r * nj:(r + 1) * nj] = out.astype(o_ref.dtype)
        else:
            out = jnp.dot(act.astype(BF16), w2_ref[0], preferred_element_type=F32)
            if normalise:
                ms = jnp.mean(out * out, axis=-1, keepdims=True)
                out = out * lax.rsqrt(ms + NORM_EPS) * kg_ref[...]
            o_ref[0, 0, r * nj:(r + 1) * nj, :] = out.astype(o_ref.dtype)


def _compress(pa, col0, pe, w1, b1, w2, kgain, *, normalise, transpose_out):
    b, t, _ = pa.shape
    nj = t // SEL_BLK
    ncp = CMP_PER_SEL * nj
    cb0 = col0 // HEAD_DIM
    oshape = (b, NSA_KV, VROWS, ncp) if transpose_out else (b, NSA_KV, ncp, HEAD_DIM)
    assert not (normalise and transpose_out)
    w2_in = (w2.T if transpose_out else w2)[None].astype(BF16)
    return pl.pallas_call(
        functools.partial(_compress_kernel, nj=nj, normalise=normalise, transpose_out=transpose_out),
        out_shape=jax.ShapeDtypeStruct(oshape, BF16),
        grid=(b, NSA_KV),
        in_specs=[
            pl.BlockSpec((1, t, HEAD_DIM), lambda bi, g: (bi, 0, cb0 + g)),
            pl.BlockSpec((1, CMP_BLK, HEAD_DIM), lambda bi, g: (0, 0, 0)),
            pl.BlockSpec((1, CMP_BLK, HEAD_DIM, CMP_HIDDEN), lambda bi, g: (0, 0, 0, 0)),
            pl.BlockSpec((1, 1, CMP_HIDDEN), lambda bi, g: (0, 0, 0)),
            pl.BlockSpec((1,) + w2_in.shape[1:], lambda bi, g: (0, 0, 0)),
            pl.BlockSpec((1, HEAD_DIM), lambda bi, g: (0, 0)),
        ],
        out_specs=pl.BlockSpec((1, 1) + oshape[2:], lambda bi, g: (bi, g, 0, 0)),
        scratch_shapes=[pltpu.VMEM((2 * CMP_PER_SEL, nj, CMP_HIDDEN), F32)],
        compiler_params=_cparams(("arbitrary", "arbitrary")),
        name="nsa_compress_v" if transpose_out else "nsa_compress_k",
    )(pa, pe[None], w1.reshape(1, CMP_BLK, HEAD_DIM, CMP_HIDDEN).astype(BF16), b1.reshape(1, 1, -1),
      w2_in, kgain.reshape(1, -1))


KVP_TT = 512


def _kv_prep_kernel(k_ref, v_ref, ka_ref, vt_ref, *, augment_keys):
    tt = v_ref.shape[1]
    vt_ref[0, 0, :HEAD_DIM, :] = v_ref[0].astype(F32).T.astype(vt_ref.dtype)
    vt_ref[0, 0, HEAD_DIM:, :] = (lax.broadcasted_iota(jnp.int32, (BF16_ROWS, tt), 0) == 0).astype(vt_ref.dtype)
    if augment_keys:
        key = pl.program_id(2) * tt + lax.broadcasted_iota(jnp.int32, (tt, 1), 0)
        lane = lax.broadcasted_iota(jnp.int32, (tt, LANE), 1)
        ka_ref[0, 0, :, :HEAD_DIM] = k_ref[0]
        ka_ref[0, 0, :, HEAD_DIM:] = (lane == (key // SEL_BLK) % LANE).astype(ka_ref.dtype)
    else:
        ka_ref[...] = jnp.zeros_like(ka_ref)


def _kv_prep(pb, k_cb, v_cb, *, augment_keys):
    b, t, _ = pb.shape
    ka_shape = (b, NSA_KV, t, 2 * HEAD_DIM) if augment_keys else (b, NSA_KV, BF16_ROWS, LANE)
    ka_block = (1, 1, KVP_TT, 2 * HEAD_DIM) if augment_keys else (1, 1, BF16_ROWS, LANE)
    ka_map = (lambda bi, g, ti: (bi, g, ti, 0)) if augment_keys else (lambda bi, g, ti: (bi, g, 0, 0))
    return pl.pallas_call(
        functools.partial(_kv_prep_kernel, augment_keys=augment_keys),
        out_shape=(jax.ShapeDtypeStruct(ka_shape, BF16),
                   jax.ShapeDtypeStruct((b, NSA_KV, VROWS, t), BF16)),
        grid=(b, NSA_KV, t // KVP_TT),
        in_specs=[pl.BlockSpec((1, KVP_TT, HEAD_DIM), lambda bi, g, ti: (bi, ti, k_cb + g)),
                  pl.BlockSpec((1, KVP_TT, HEAD_DIM), lambda bi, g, ti: (bi, ti, v_cb + g))],
        out_specs=(pl.BlockSpec(ka_block, ka_map),
                   pl.BlockSpec((1, 1, VROWS, KVP_TT), lambda bi, g, ti: (bi, g, 0, ti))),
        compiler_params=_cparams(("arbitrary", "arbitrary", "arbitrary")),
        name="kv_prep_aug" if augment_keys else "kv_prep",
    )(pb, pb)


def _query_positions(start, nq):
    col = lax.broadcasted_iota(jnp.int32, (1, NSA_HPG * nq), 1)
    return start + col % nq


def _transpose_queries(q):
    qf = q.astype(F32)
    return jnp.concatenate([qf[:, h * HEAD_DIM:(h + 1) * HEAD_DIM].T for h in range(NSA_HPG)], axis=1)


def _normed_heads_out(o_ref, y, gain, nq):
    ms = jnp.mean(y * y, axis=0, keepdims=True)
    y = y * lax.rsqrt(ms + NORM_EPS) * gain
    for h in range(NSA_HPG):
        o_ref[0, :, h * HEAD_DIM:(h + 1) * HEAD_DIM] = y[:, h * nq:(h + 1) * nq].T.astype(o_ref.dtype)


def _nsa_cmp_kernel(q_ref, kc_ref, vct_ref, o_ref, qs_ref, *, nq, nj, nselp, n_top):
    start = pl.program_id(2) * nq
    qt = _transpose_queries(q_ref[0]).astype(BF16)
    s = jnp.dot(kc_ref[0, 0], qt, preferred_element_type=F32)
    ncp = CMP_PER_SEL * nj
    row = lax.broadcasted_iota(jnp.int32, (ncp, 1), 0)
    cmp_end = SEL_BLK * (row % nj) + CMP_STRIDE * (row // nj) + (CMP_BLK - 1)
    tpos = _query_positions(start, nq)
    s = jnp.where(cmp_end <= tpos, s, NEG)
    m = jnp.max(s, axis=0, keepdims=True)
    m = jnp.where(m <= 0.5 * NEG, 0.0, m)
    p = jnp.exp2(s - m)
    pv = jnp.dot(vct_ref[0, 0], p.astype(BF16), preferred_element_type=F32)
    inv = 1.0 / jnp.maximum(pv[HEAD_DIM:HEAD_DIM + 1, :], 1e-30)
    o_ref[0, 0, 0] = pv[:HEAD_DIM, :] * inv

    imp = p[:, 0:nq] * inv[:, 0:nq]
    for h in range(1, NSA_HPG):
        imp = imp + p[:, h * nq:(h + 1) * nq] * inv[:, h * nq:(h + 1) * nq]
    last = imp[(CMP_PER_SEL - 1) * nj:, :]
    jrow = lax.broadcasted_iota(jnp.int32, (nj, 1), 0)
    prev = jnp.where(jrow == 0, 0.0, pltpu.roll(last, 1, 0))
    imp_sel = prev + last
    for r in range(CMP_PER_SEL - 1):
        imp_sel = imp_sel + imp[r * nj:(r + 1) * nj, :]

    tq = tpos[:, :nq]
    cur = tq // SEL_BLK
    forced = (jrow == 0) | (jrow == cur) | (jrow == cur - 1)
    score = jnp.where(forced, -NEG, jnp.where(jrow <= cur, imp_sel, NEG))
    sel = jnp.zeros((nj, nq), F32)
    for _ in range(n_top):
        mx = jnp.max(score, axis=0, keepdims=True)
        idx = jnp.min(jnp.where(score == mx, jrow, nj), axis=0, keepdims=True)
        hit = jrow == idx
        sel = jnp.where(hit & (mx > 0.5 * NEG), 1.0, sel)
        score = jnp.where(hit, 2.0 * NEG, score)
    selneg = jnp.where(sel > 0.0, 0.0, NEG)
    if nselp > nj:
        selneg = jnp.concatenate([selneg, jnp.full((nselp - nj, nq), NEG, F32)], axis=0)
    selneg = jnp.concatenate([selneg] * NSA_HPG, axis=1).astype(BF16)
    for hf in range(nselp // LANE):
        qs_ref[0, 0, 0, 2 * LANE * hf:2 * LANE * hf + LANE, :] = qt
        qs_ref[0, 0, 0, 2 * LANE * hf + LANE:2 * LANE * (hf + 1), :] = selneg[hf * LANE:(hf + 1) * LANE, :]


def _nsa_cmp(pb, kc, vct, t):
    b = pb.shape[0]
    nq = QBLK
    nqb = t // nq
    nj = t // SEL_BLK
    nselp = -(-nj // LANE) * LANE
    n_top = min(SEL_TOP, nj)
    ncp = CMP_PER_SEL * nj
    return pl.pallas_call(
        functools.partial(_nsa_cmp_kernel, nq=nq, nj=nj, nselp=nselp, n_top=n_top),
        out_shape=(jax.ShapeDtypeStruct((b, NSA_KV, nqb, HEAD_DIM, NSA_HPG * nq), F32),
                   jax.ShapeDtypeStruct((b, NSA_KV, nqb, 2 * nselp, NSA_HPG * nq), BF16)),
        grid=(b, NSA_KV, nqb),
        in_specs=[
            pl.BlockSpec((1, nq, NSA_HPG * HEAD_DIM), lambda bi, g, qi: (bi, qi, g)),
            pl.BlockSpec((1, 1, ncp, HEAD_DIM), lambda bi, g, qi: (bi, g, 0, 0)),
            pl.BlockSpec((1, 1, VROWS, ncp), lambda bi, g, qi: (bi, g, 0, 0)),
        ],
        out_specs=(pl.BlockSpec((1, 1, 1, HEAD_DIM, NSA_HPG * nq), lambda bi, g, qi: (bi, g, qi, 0, 0)),
                   pl.BlockSpec((1, 1, 1, 2 * nselp, NSA_HPG * nq), lambda bi, g, qi: (bi, g, qi, 0, 0))),
        compiler_params=_cparams(("arbitrary", "arbitrary", "arbitrary")),
        name="nsa_compressed_topk",
    )(pb, kc, vct)


SLC_QB = 2


def _nsa_slc_kernel(qs_ref, ka_ref, vt_ref, o_ref, acc_ref, sa_ref, sb_ref, p_ref, m_ref, *, nq):
    qi = pl.program_id(2)
    span = SLC_QB * nq
    start = qi * span
    bcol = NSA_HPG * nq
    ncol = SLC_QB * bcol
    tiles_per_half = LANE * SEL_BLK // KVT
    n_diag = span // KVT
    n_past = qi * n_diag
    krow = lax.broadcasted_iota(jnp.int32, (KVT, 1), 0)
    lane = lax.broadcasted_iota(jnp.int32, (1, LANE), 1)
    acc_ref[...] = jnp.zeros_like(acc_ref)
    m_ref[...] = jnp.full_like(m_ref, NEG)

    def scores(ti, dst_ref):
        k0 = pl.multiple_of(ti * KVT, KVT)
        r0 = pl.multiple_of((ti // tiles_per_half) * 2 * LANE, 2 * LANE)
        ka = ka_ref[0, 0, pl.ds(k0, KVT), :]
        for qb in range(SLC_QB):
            dst_ref[:, qb * bcol:(qb + 1) * bcol] = jnp.dot(ka, qs_ref[0, 0, qb, pl.ds(r0, 2 * LANE), :],
                                                            preferred_element_type=F32)

    def softmax_pv(src_ref, ti, diagonal):
        k0 = pl.multiple_of(ti * KVT, KVT)
        for c in range(ncol // LANE):
            cs = slice(c * LANE, (c + 1) * LANE)
            s = src_ref[:, cs]
            if diagonal:
                q0 = (c * LANE // bcol) * nq + (c * LANE) % nq
                s = jnp.where(k0 + krow <= start + q0 + lane, s, NEG)
            m = m_ref[:, cs]
            mn = jnp.maximum(m, jnp.max(s, axis=0, keepdims=True))
            p_ref[:, cs] = jnp.exp2(s - mn).astype(BF16)
            m_ref[:, cs] = mn
            acc_ref[:, cs] = acc_ref[:, cs] * jnp.exp2(m - mn)
        acc_ref[...] += jnp.dot(vt_ref[0, 0, :, pl.ds(k0, KVT)], p_ref[...], preferred_element_type=F32)

    scores(0, sa_ref)

    def pair(pi, carry):
        t0 = 2 * pi
        scores(t0 + 1, sb_ref)
        softmax_pv(sa_ref, t0, False)
        scores(t0 + 2, sa_ref)
        softmax_pv(sb_ref, t0 + 1, False)
        return carry

    lax.fori_loop(0, n_past // 2, pair, 0)
    bufs = (sa_ref, sb_ref)
    for dt in range(n_diag):
        if dt + 1 < n_diag:
            scores(n_past + dt + 1, bufs[(dt + 1) % 2])
        softmax_pv(bufs[dt % 2], n_past + dt, True)

    for qb in range(SLC_QB):
        cs = slice(qb * bcol, (qb + 1) * bcol)
        o_ref[0, 0, qb] = acc_ref[:HEAD_DIM, cs] / acc_ref[HEAD_DIM:HEAD_DIM + 1, cs]


def _nsa_slc(qsel, ka, vt, t):
    b = qsel.shape[0]
    nq = QBLK
    nqb = t // nq
    rows = qsel.shape[3]
    bcol = NSA_HPG * nq
    ncol = SLC_QB * bcol
    assert nqb % SLC_QB == 0 and (SLC_QB * nq // KVT) % 2 == 0
    return pl.pallas_call(
        functools.partial(_nsa_slc_kernel, nq=nq),
        out_shape=jax.ShapeDtypeStruct((b, NSA_KV, nqb, HEAD_DIM, bcol), F32),
        grid=(b, NSA_KV, nqb // SLC_QB),
        in_specs=[
            pl.BlockSpec((1, 1, SLC_QB, rows, bcol), lambda bi, g, qi: (bi, g, qi, 0, 0)),
            pl.BlockSpec((1, 1, t, 2 * HEAD_DIM), lambda bi, g, qi: (bi, g, 0, 0)),
            pl.BlockSpec((1, 1, VROWS, t), lambda bi, g, qi: (bi, g, 0, 0)),
        ],
        out_specs=pl.BlockSpec((1, 1, SLC_QB, HEAD_DIM, bcol), lambda bi, g, qi: (bi, g, qi, 0, 0)),
        scratch_shapes=[pltpu.VMEM((VROWS, ncol), F32), pltpu.VMEM((KVT, ncol), F32),
                        pltpu.VMEM((KVT, ncol), F32), pltpu.VMEM((KVT, ncol), BF16),
                        pltpu.VMEM((1, ncol), F32)],
        compiler_params=_cparams(("arbitrary", "arbitrary", "arbitrary")),
        name="nsa_selected",
    )(qsel, ka, vt)


def _band_scores(qt, k_ref, vt_ref, start, nq, nk, window, t_len, sink):
    k0 = pl.multiple_of(jnp.clip(start + nq - nk, 0, t_len - nk), LANE)
    s = jnp.dot(k_ref[0, pl.ds(k0, nk), :], qt, preferred_element_type=F32)
    rel = _query_positions(start, nq) - (k0 + lax.broadcasted_iota(jnp.int32, (nk, 1), 0))
    s = jnp.where((rel >= 0) & (rel < window), s, NEG)
    m = jnp.max(s, axis=0, keepdims=True)
    if sink is not None:
        m = jnp.maximum(m, sink)
    p = jnp.exp2(s - m).astype(BF16)
    pv = jnp.dot(vt_ref[0, 0, :, pl.ds(k0, nk)], p, preferred_element_type=F32)
    den = pv[HEAD_DIM:HEAD_DIM + 1, :]
    if sink is not None:
        den = den + jnp.exp2(sink - m)
    return pv[:HEAD_DIM, :] / den


def _band_keys(window):
    return -(-(window - 1) // LANE) * LANE + QBLK


def _nsa_finish_kernel(qs_ref, k_ref, vt_ref, oc_ref, os_ref, gl_ref, gain_ref, o_ref, gt_ref, *, nq, t_len):
    g = pl.program_id(1)
    start = pl.program_id(2) * nq
    o_win = _band_scores(qs_ref[0, 0, 0], k_ref, vt_ref, start, nq, _band_keys(NSA_WIN), NSA_WIN, t_len, None)
    gt_ref[...] = jax.nn.sigmoid(gl_ref[...]).T
    branch = []
    for br in range(3):
        rows = [gt_ref[pl.ds((g * NSA_HPG + h) * 3 + br, 1), :] for h in range(NSA_HPG)]
        branch.append(jnp.concatenate(rows, axis=1))
    y = branch[0] * oc_ref[0, 0, 0] + branch[1] * os_ref[0, 0, 0] + branch[2] * o_win
    _normed_heads_out(o_ref, y, gain_ref[0], nq)


def _nsa_finish(qsel, pb, k_cb, vt, o_cmp, o_slc, pa2, gate_cb, gain_t, t):
    b = pb.shape[0]
    nq = QBLK
    nqb = t // nq
    ncol = NSA_HPG * nq
    blk5 = pl.BlockSpec((1, 1, 1, HEAD_DIM, ncol), lambda bi, g, qi: (bi, g, qi, 0, 0))
    return pl.pallas_call(
        functools.partial(_nsa_finish_kernel, nq=nq, t_len=t),
        out_shape=jax.ShapeDtypeStruct((b, t, NSA_HEADS * HEAD_DIM), BF16),
        grid=(b, NSA_KV, nqb),
        in_specs=[
            blk5,
            pl.BlockSpec((1, t, HEAD_DIM), lambda bi, g, qi: (bi, 0, k_cb + g)),
            pl.BlockSpec((1, 1, VROWS, t), lambda bi, g, qi: (bi, g, 0, 0)),
            blk5, blk5,
            pl.BlockSpec((nq, LANE), lambda bi, g, qi: (bi * nqb + qi, gate_cb)),
            pl.BlockSpec((1, HEAD_DIM, ncol), lambda bi, g, qi: (g, 0, 0)),
        ],
        out_specs=pl.BlockSpec((1, nq, NSA_HPG * HEAD_DIM), lambda bi, g, qi: (bi, qi, g)),
        scratch_shapes=[pltpu.VMEM((LANE, nq), F32)],
        compiler_params=_cparams(("arbitrary", "arbitrary", "arbitrary")),
        name="nsa_window_finish",
    )(qsel, pb, vt, o_cmp, o_slc, pa2, gain_t)


def _swa_kernel(q_ref, k_ref, vt_ref, sink_ref, gain_ref, o_ref, *, nq, t_len):
    start = pl.program_id(2) * nq
    qt = _transpose_queries(q_ref[0]).astype(BF16)
    y = _band_scores(qt, k_ref, vt_ref, start, nq, _band_keys(SWA_WIN), SWA_WIN, t_len, sink_ref[0])
    _normed_heads_out(o_ref, y, gain_ref[0], nq)


def _swa(pb, q_cb, k_cb, vt, sink_rows, gain_t, t):
    b = pb.shape[0]
    nq = QBLK
    nqb = t // nq
    ncol = SWA_HPG * nq
    return pl.pallas_call(
        functools.partial(_swa_kernel, nq=nq, t_len=t),
        out_shape=jax.ShapeDtypeStruct((b, t, SWA_HEADS * HEAD_DIM), BF16),
        grid=(b, SWA_KV, nqb),
        in_specs=[
            pl.BlockSpec((1, nq, SWA_HPG * HEAD_DIM), lambda bi, g, qi: (bi, qi, q_cb + g)),
            pl.BlockSpec((1, t, HEAD_DIM), lambda bi, g, qi: (bi, 0, k_cb + g)),
            pl.BlockSpec((1, 1, VROWS, t), lambda bi, g, qi: (bi, g, 0, 0)),
            pl.BlockSpec((1, 1, ncol), lambda bi, g, qi: (g, 0, 0)),
            pl.BlockSpec((1, HEAD_DIM, ncol), lambda bi, g, qi: (g, 0, 0)),
        ],
        out_specs=pl.BlockSpec((1, nq, SWA_HPG * HEAD_DIM), lambda bi, g, qi: (bi, qi, g)),
        compiler_params=_cparams(("arbitrary", "arbitrary", "arbitrary")),
        name="swa_sink",
    )(pb, pb, vt, sink_rows, gain_t)


def _outproj_kernel(ya_ref, yb_ref, yc_ref, wa_ref, wb_ref, wc_ref, x_ref, gm_ref, o_ref):
    y = jnp.dot(ya_ref[...], wa_ref[...], preferred_element_type=F32)
    y = y + jnp.dot(yb_ref[...], wb_ref[...], preferred_element_type=F32)
    y = y + jnp.dot(yc_ref[...], wc_ref[...], preferred_element_type=F32)
    o_ref[...] = x_ref[...] + gm_ref[0] * y


def _outproj(ya, yb, yc, w_out, x2, gm, *, rows_per_batch):
    n, d = x2.shape
    tm, tn = 1024, 512
    bpb = rows_per_batch // tm
    wa, wb, wc = ya.shape[1], yb.shape[1], yc.shape[1]
    w = w_out.astype(BF16)
    return pl.pallas_call(
        _outproj_kernel,
        out_shape=jax.ShapeDtypeStruct((n, d), F32),
        grid=(n // tm, d // tn),
        in_specs=[
            pl.BlockSpec((tm, wa), lambda i, j: (i, 0)),
            pl.BlockSpec((tm, wb), lambda i, j: (i, 0)),
            pl.BlockSpec((tm, wc), lambda i, j: (i, 0)),
            pl.BlockSpec((wa, tn), lambda i, j: (0, j)),
            pl.BlockSpec((wb, tn), lambda i, j: (0, j)),
            pl.BlockSpec((wc, tn), lambda i, j: (0, j)),
            pl.BlockSpec((tm, tn), lambda i, j: (i, j)),
            pl.BlockSpec((1, 1, tn), lambda i, j: (i // bpb, 0, j)),
        ],
        out_specs=pl.BlockSpec((tm, tn), lambda i, j: (i, j)),
        compiler_params=_cparams(("arbitrary", "arbitrary")),
        name="mix_outproj",
    )(ya, yb, yc, w[:wa], w[wa:wa + wb], w[wa + wb:], x2, gm)


def _ffn_kernel(x_ref, g_ref, sc_ref, sh_ref, gf_ref, wg_ref, wu_ref, wd_ref, o_ref, h_ref):
    f = pl.program_id(1)

    @pl.when(f == 0)
    def _():
        h_ref[...] = _mod_rmsnorm(x_ref[...], g_ref[...], sc_ref[0], sh_ref[0]).astype(BF16)
        o_ref[...] = jnp.zeros_like(o_ref)

    h = h_ref[...]
    a = jnp.dot(h, wg_ref[...], preferred_element_type=F32)
    u = jnp.dot(h, wu_ref[...], preferred_element_type=F32)
    act = (a * jax.nn.sigmoid(a) * u).astype(BF16)
    o_ref[...] += jnp.dot(act, wd_ref[...], preferred_element_type=F32)

    @pl.when(f == pl.num_programs(1) - 1)
    def _():
        o_ref[...] = x_ref[...] + gf_ref[0] * o_ref[...]


def _ffn_dense(x2, g, sc, sh, gf, wg, wu, wd, *, rows_per_batch):
    n, d = x2.shape
    ff = wg.shape[1]
    tm, tf = 512, 512
    bpb = rows_per_batch // tm
    mod_spec = pl.BlockSpec((1, 1, d), lambda i, f: (i // bpb, 0, 0))
    return pl.pallas_call(
        _ffn_kernel,
        out_shape=jax.ShapeDtypeStruct((n, d), F32),
        grid=(n // tm, ff // tf),
        in_specs=[
            pl.BlockSpec((tm, d), lambda i, f: (i, 0)),
            pl.BlockSpec((1, d), lambda i, f: (0, 0)),
            mod_spec, mod_spec, mod_spec,
            pl.BlockSpec((d, tf), lambda i, f: (0, f)),
            pl.BlockSpec((d, tf), lambda i, f: (0, f)),
            pl.BlockSpec((tf, d), lambda i, f: (f, 0)),
        ],
        out_specs=pl.BlockSpec((tm, d), lambda i, f: (i, 0)),
        scratch_shapes=[pltpu.VMEM((tm, d), BF16)],
        compiler_params=_cparams(("arbitrary", "arbitrary")),
        name="ffn_dense",
    )(x2, g, sc, sh, gf, wg, wu, wd)


def _router_kernel(x_ref, g_ref, sc_ref, sh_ref, wr_ref, h_ref, idx_ref, wt_ref):
    h = _mod_rmsnorm(x_ref[...], g_ref[...], sc_ref[0], sh_ref[0])
    h_ref[...] = h
    logits = jnp.dot(h, wr_ref[...], precision=HIGHEST, preferred_element_type=F32)
    e = lax.broadcasted_iota(jnp.int32, logits.shape, 1)
    m1 = jnp.max(logits, axis=-1, keepdims=True)
    i1 = jnp.min(jnp.where(logits == m1, e, N_EXPERTS), axis=-1, keepdims=True)
    rest = jnp.where(e == i1, -jnp.inf, logits)
    m2 = jnp.max(rest, axis=-1, keepdims=True)
    i2 = jnp.min(jnp.where(rest == m2, e, N_EXPERTS), axis=-1, keepdims=True)
    e2 = jnp.exp(m2 - m1)
    den = 1.0 + e2
    idx_ref[:, 0:1] = i1
    idx_ref[:, 1:2] = i2
    wt_ref[:, 0:1] = 1.0 / den
    wt_ref[:, 1:2] = e2 / den


def _router(x2, g, sc, sh, w_router, *, rows_per_batch):
    n, d = x2.shape
    tm = 512
    bpb = rows_per_batch // tm
    mod_spec = pl.BlockSpec((1, 1, d), lambda i: (i // bpb, 0, 0))
    return pl.pallas_call(
        _router_kernel,
        out_shape=(jax.ShapeDtypeStruct((n, d), F32),
                   jax.ShapeDtypeStruct((n, TOP_K), jnp.int32),
                   jax.ShapeDtypeStruct((n, TOP_K), F32)),
        grid=(n // tm,),
        in_specs=[
            pl.BlockSpec((tm, d), lambda i: (i, 0)),
            pl.BlockSpec((1, d), lambda i: (0, 0)),
            mod_spec, mod_spec,
            pl.BlockSpec((d, N_EXPERTS), lambda i: (0, 0)),
        ],
        out_specs=(pl.BlockSpec((tm, d), lambda i: (i, 0)),
                   pl.BlockSpec((tm, TOP_K), lambda i: (i, 0)),
                   pl.BlockSpec((tm, TOP_K), lambda i: (i, 0))),
        compiler_params=_cparams(("arbitrary",)),
        name="moe_router",
    )(x2, g, sc, sh, w_router)


DMA_UNROLL = 8


def _expert_kernel(be_ref, src_ref, nused_ref, h_hbm, wg_ref, wu_ref, wd_ref, o_ref, xb_ref, sem):
    i = pl.program_id(0)
    f = pl.program_id(1)
    used = i < nused_ref[0]

    def row_copy(r):
        return pltpu.make_async_copy(h_hbm.at[pl.ds(src_ref[i * MOE_TM + r], 1), :],
                                     xb_ref.at[pl.ds(r, 1), :], sem)

    @pl.when(f == 0)
    def _():
        o_ref[...] = jnp.zeros_like(o_ref)

    @pl.when((f == 0) & used)
    def _():
        lax.fori_loop(0, MOE_TM, lambda r, c: (row_copy(r).start(), c)[1], 0, unroll=DMA_UNROLL)
        lax.fori_loop(0, MOE_TM, lambda r, c: (row_copy(r).wait(), c)[1], 0, unroll=DMA_UNROLL)

    @pl.when(used)
    def _():
        xb = xb_ref[...]
        a = jnp.dot(xb, wg_ref[0], preferred_element_type=F32)
        u = jnp.dot(xb, wu_ref[0], preferred_element_type=F32)
        act = (a * jax.nn.sigmoid(a) * u).astype(BF16)
        o_ref[...] += jnp.dot(act, wd_ref[0], preferred_element_type=F32)


def _experts(h_tok, block_expert, src_row, n_used, wg, wu, wd, n_blocks):
    n, d = h_tok.shape
    ff = wg.shape[2]
    tf = 512
    return pl.pallas_call(
        _expert_kernel,
        out_shape=jax.ShapeDtypeStruct((n_blocks * MOE_TM, d), F32),
        grid_spec=pltpu.PrefetchScalarGridSpec(
            num_scalar_prefetch=3,
            grid=(n_blocks, ff // tf),
            in_specs=[
                pl.BlockSpec(memory_space=pl.ANY),
                pl.BlockSpec((1, d, tf), lambda i, f, be, src, nu: (be[i], 0, f)),
                pl.BlockSpec((1, d, tf), lambda i, f, be, src, nu: (be[i], 0, f)),
                pl.BlockSpec((1, tf, d), lambda i, f, be, src, nu: (be[i], f, 0)),
            ],
            out_specs=pl.BlockSpec((MOE_TM, d), lambda i, f, be, src, nu: (i, 0)),
            scratch_shapes=[pltpu.VMEM((MOE_TM, d), F32), pltpu.SemaphoreType.DMA],
        ),
        compiler_params=_cparams(("arbitrary", "arbitrary")),
        name="moe_experts",
    )(block_expert, src_row, n_used, h_tok, wg, wu, wd)


CMB_TM = 256


def _combine_kernel(pos_ref, y_hbm, x_ref, wt_ref, gf_ref, o_ref, yb_ref, sem):
    i = pl.program_id(0)

    def row_copy(r, k):
        return pltpu.make_async_copy(y_hbm.at[pl.ds(pos_ref[(i * CMB_TM + r) * TOP_K + k], 1), :],
                                     yb_ref.at[k, pl.ds(r, 1), :], sem)

    def start(r, c):
        for k in range(TOP_K):
            row_copy(r, k).start()
        return c

    def wait(r, c):
        for k in range(TOP_K):
            row_copy(r, k).wait()
        return c

    lax.fori_loop(0, CMB_TM, start, 0, unroll=DMA_UNROLL)
    lax.fori_loop(0, CMB_TM, wait, 0, unroll=DMA_UNROLL)
    wt = wt_ref[...]
    f = yb_ref[0] * wt[:, 0:1]
    for k in range(1, TOP_K):
        f = f + yb_ref[k] * wt[:, k:k + 1]
    o_ref[...] = x_ref[...] + gf_ref[0] * f


def _combine(pos, y_buf, x2, wt, gf, *, rows_per_batch):
    n, d = x2.shape
    bpb = rows_per_batch // CMB_TM
    return pl.pallas_call(
        _combine_kernel,
        out_shape=jax.ShapeDtypeStruct((n, d), F32),
        grid_spec=pltpu.PrefetchScalarGridSpec(
            num_scalar_prefetch=1,
            grid=(n // CMB_TM,),
            in_specs=[
                pl.BlockSpec(memory_space=pl.ANY),
                pl.BlockSpec((CMB_TM, d), lambda i, pos: (i, 0)),
                pl.BlockSpec((CMB_TM, TOP_K), lambda i, pos: (i, 0)),
                pl.BlockSpec((1, 1, d), lambda i, pos: (i // bpb, 0, 0)),
            ],
            out_specs=pl.BlockSpec((CMB_TM, d), lambda i, pos: (i, 0)),
            scratch_shapes=[pltpu.VMEM((TOP_K, CMB_TM, d), F32), pltpu.SemaphoreType.DMA],
        ),
        compiler_params=_cparams(("arbitrary",)),
        name="moe_combine",
    )(pos, y_buf, x2, wt, gf)


def _moe(x2, g, sc, sh, gf, w_router, wg, wu, wd, *, rows_per_batch):
    n, d = x2.shape
    h_tok, top_idx, top_w = _router(x2, g, sc, sh, w_router, rows_per_batch=rows_per_batch)
    n_assign = n * TOP_K
    n_blocks = n_assign // MOE_TM + N_EXPERTS
    expert_of = top_idx.reshape(-1)
    onehot = (expert_of[:, None] == jnp.arange(N_EXPERTS)[None, :]).astype(jnp.int32)
    rank = jnp.take_along_axis(jnp.cumsum(onehot, axis=0) - onehot, expert_of[:, None], axis=1)[:, 0]
    counts = jnp.sum(onehot, axis=0)
    padded = (counts + MOE_TM - 1) // MOE_TM * MOE_TM
    pad_ends = jnp.cumsum(padded)
    pad_starts = pad_ends - padded
    dest = pad_starts[expert_of] + rank
    token_of = jnp.arange(n_assign, dtype=jnp.int32) // TOP_K
    src_row = jnp.zeros((n_blocks * MOE_TM,), jnp.int32).at[dest].set(token_of)
    block_start = jnp.arange(n_blocks, dtype=jnp.int32) * MOE_TM
    block_expert = jnp.minimum(jnp.sum(block_start[:, None] >= pad_ends[None, :], axis=1),
                               N_EXPERTS - 1).astype(jnp.int32)
    n_used = (pad_ends[-1] // MOE_TM).astype(jnp.int32).reshape(1)
    y_buf = _experts(h_tok, block_expert, src_row, n_used, wg, wu, wd, n_blocks)
    return _combine(dest.astype(jnp.int32), y_buf, x2, top_w, gf, rows_per_batch=rows_per_batch)


PA_KC = 2 * CONV_CH
PA_VC = PA_KC + NSA_KV * HEAD_DIM
PA_GATE = PA_VC + NSA_KV * HEAD_DIM
PA_WIDTH = PA_GATE + LANE
PB_QN = 0
PB_QS = NSA_HEADS * HEAD_DIM
PB_KS = PB_QS + SWA_HEADS * HEAD_DIM
PB_VS = PB_KS + NSA_KV * HEAD_DIM
PB_KW = PB_VS + NSA_KV * HEAD_DIM
PB_VW = PB_KW + NSA_KV * HEAD_DIM
PB_KC = PB_VW + NSA_KV * HEAD_DIM
PB_VC = PB_KC + SWA_KV * HEAD_DIM
PB_WIDTH = PB_VC + SWA_KV * HEAD_DIM


def _split_w_in(w_in, nsa_q_g, nsa_k_g, swa_q_g, swa_k_g):
    sizes = [2 * CONV_CH, NSA_HEADS * HEAD_DIM] + [NSA_KV * HEAD_DIM] * 6 + [
        3 * NSA_HEADS, SWA_HEADS * HEAD_DIM, SWA_KV * HEAD_DIM, SWA_KV * HEAD_DIM]
    offs = [0]
    for s in sizes:
        offs.append(offs[-1] + s)
    seg = [w_in[:, offs[i]:offs[i + 1]] for i in range(len(sizes))]
    (u_conv, q_n, kc, vc, ks, vs, kw, vw, g_n, q_s, k_s, v_s) = seg
    d = w_in.shape[0]
    g_pad = jnp.zeros((d, LANE - 3 * NSA_HEADS), w_in.dtype)
    wa = jnp.concatenate([u_conv, kc, vc, g_n, g_pad], axis=1).astype(BF16)
    wb = jnp.concatenate([q_n, q_s, ks, vs, kw, vw, k_s, v_s], axis=1).astype(BF16)
    qscale = HEAD_DIM ** -0.5 * LOG2E
    ones = jnp.ones((HEAD_DIM,), F32)

    def rep(v, n):
        return jnp.tile(v.astype(F32), n)

    gain = jnp.concatenate([rep(nsa_q_g * qscale, NSA_HEADS), rep(swa_q_g * qscale, SWA_HEADS),
                            rep(nsa_k_g[1], NSA_KV), rep(ones, NSA_KV), rep(nsa_k_g[2], NSA_KV),
                            rep(ones, NSA_KV), rep(swa_k_g, SWA_KV), rep(ones, SWA_KV)]).reshape(1, -1)
    one = jnp.ones((HEAD_DIM,), F32)
    zero = jnp.zeros((HEAD_DIM,), F32)
    flag = jnp.concatenate([rep(one, NSA_HEADS + SWA_HEADS), rep(one, NSA_KV), rep(zero, NSA_KV),
                            rep(one, NSA_KV), rep(zero, NSA_KV), rep(one, SWA_KV),
                            rep(zero, SWA_KV)]).reshape(1, -1)
    return wa, wb, gain, flag


def _head_gain_t(gain, n_groups, hpg):
    gt = gain.astype(F32).reshape(n_groups, hpg, HEAD_DIM).transpose(0, 2, 1)
    return jnp.repeat(gt, QBLK, axis=2)


def _mixer(x2, b, t, norm_g, sc, sh, gm, w_in, conv_dw_w, conv_dw_b, conv_ln_g, conv_ln_b, conv_pw_w, cmp_pe,
           cmp_w1, cmp_b1, cmp_w2, nsa_q_g, nsa_k_g, swa_q_g, swa_k_g, swa_sinks, grp_out_g, w_out):
    n, d = x2.shape
    wa, wb, gain, flag = _split_w_in(w_in, nsa_q_g, nsa_k_g, swa_q_g, swa_k_g)
    dummy = jnp.zeros((1, PA_WIDTH), F32)
    pa = _inproj(x2, norm_g, sc, sh, wa, dummy, dummy, rows_per_batch=t, tm=512, tn=PA_WIDTH,
                 out_dtype=F32, epilogue=False, name="inproj_f32")
    pb = _inproj(x2, norm_g, sc, sh, wb, gain, flag, rows_per_batch=t, tm=1024, tn=512,
                 out_dtype=BF16, epilogue=True, name="inproj_bf16")
    pa3 = pa.reshape(b, t, PA_WIDTH)
    pb3 = pb.reshape(b, t, PB_WIDTH)
    og_a = grp_out_g[:CONV_CH]
    og_b = grp_out_g[CONV_CH:CONV_CH + NSA_HEADS * HEAD_DIM]
    og_c = grp_out_g[CONV_CH + NSA_HEADS * HEAD_DIM:]

    y_a = _conformer_conv(pa3, conv_dw_w, conv_dw_b, conv_ln_g, conv_ln_b, conv_pw_w, og_a)

    k_cmp = _compress(pa3, PA_KC, cmp_pe[0], cmp_w1[0], cmp_b1[0], cmp_w2[0], nsa_k_g[0],
                      normalise=True, transpose_out=False)
    v_cmp_t = _compress(pa3, PA_VC, cmp_pe[1], cmp_w1[1], cmp_b1[1], cmp_w2[1], nsa_k_g[0],
                        normalise=False, transpose_out=True)
    ka_slc, vt_slc = _kv_prep(pb3, PB_KS // HEAD_DIM, PB_VS // HEAD_DIM, augment_keys=True)
    _, vt_win = _kv_prep(pb3, PB_KW // HEAD_DIM, PB_VW // HEAD_DIM, augment_keys=False)
    _, vt_swa = _kv_prep(pb3, PB_KC // HEAD_DIM, PB_VC // HEAD_DIM, augment_keys=False)
    o_cmp, qsel = _nsa_cmp(pb3, k_cmp, v_cmp_t, t)
    o_slc = _nsa_slc(qsel, ka_slc, vt_slc, t)
    y_b = _nsa_finish(qsel, pb3, PB_KW // HEAD_DIM, vt_win, o_cmp, o_slc, pa, PA_GATE // HEAD_DIM,
                      _head_gain_t(og_b, NSA_KV, NSA_HPG), t)

    sink_rows = jnp.repeat((swa_sinks.astype(F32) * LOG2E).reshape(SWA_KV, SWA_HPG), QBLK, axis=1).reshape(
        SWA_KV, 1, SWA_HPG * QBLK)
    y_c = _swa(pb3, PB_QS // (SWA_HPG * HEAD_DIM), PB_KC // HEAD_DIM, vt_swa, sink_rows,
               _head_gain_t(og_c, SWA_KV, SWA_HPG), t)

    return _outproj(y_a.reshape(n, -1), y_b.reshape(n, -1), y_c.reshape(n, -1), w_out, x2, gm, rows_per_batch=t)


def kernel(x, c, norm_mix_g, norm_ffn_g, w_ada, b_ada, w_in, conv_dw_w, conv_dw_b, conv_ln_g, conv_ln_b, conv_pw_w, cmp_pe, cmp_w1, cmp_b1, cmp_w2, nsa_q_g, nsa_k_g, swa_q_g, swa_k_g, swa_sinks, grp_out_g, w_out, ffn_w_gate, ffn_w_up, ffn_w_down, moe_router, moe_w_gate, moe_w_up, moe_w_down):
    b, t, d = x.shape
    depth = w_in.shape[0]
    mod = _adaln(c, w_ada, b_ada)
    x2 = x.reshape(b * t, d)
    for i in range(depth):
        sh_m, sc_m, g_m, sh_f, sc_f, g_f = [mod[i, :, k, :].reshape(b, 1, d) for k in range(N_ADA)]
        x2 = _mixer(x2, b, t, norm_mix_g[i].reshape(1, d), sc_m, sh_m, g_m, w_in[i], conv_dw_w[i], conv_dw_b[i],
                    conv_ln_g[i], conv_ln_b[i], conv_pw_w[i], cmp_pe[i], cmp_w1[i], cmp_b1[i], cmp_w2[i],
                    nsa_q_g[i], nsa_k_g[i], swa_q_g[i], swa_k_g[i], swa_sinks[i], grp_out_g[i], w_out[i])
        j = i // 2
        gf = norm_ffn_g[i].reshape(1, d)
        if i % 2 == 0:
            x2 = _ffn_dense(x2, gf, sc_f, sh_f, g_f, ffn_w_gate[j].astype(BF16), ffn_w_up[j].astype(BF16),
                            ffn_w_down[j].astype(BF16), rows_per_batch=t)
        else:
            x2 = _moe(x2, gf, sc_f, sh_f, g_f, moe_router[j], moe_w_gate[j].astype(BF16),
                      moe_w_up[j].astype(BF16), moe_w_down[j].astype(BF16), rows_per_batch=t)
    return x2.reshape(b, t, d)
```

```python
import functools
import math

import jax
import jax.numpy as jnp
from jax import lax
from jax.experimental import pallas as pl
from jax.experimental.pallas import tpu as pltpu

F32 = jnp.float32
BF16 = jnp.bfloat16
HIGHEST = lax.Precision.HIGHEST

HEAD_DIM = 128
CONV_CH = 512
CONV_K = 31
NSA_HEADS = 6
NSA_KV = 2
NSA_HPG = NSA_HEADS // NSA_KV
CMP_BLK = 32
CMP_STRIDE = 16
CMP_HIDDEN = 256
SEL_BLK = 64
SEL_TOP = 16
CMP_PER_SEL = SEL_BLK // CMP_STRIDE
NSA_WIN = 512
SWA_HEADS = 6
SWA_KV = 2
SWA_HPG = SWA_HEADS // SWA_KV
SWA_WIN = 128
N_EXPERTS = 8
TOP_K = 2
N_ADA = 6
NORM_EPS = 1e-6
LN_EPS = 1e-5
LOG2E = math.log2(math.e)

LANE = 128
SUBLANE = 8
BF16_ROWS = 16
VMEM_LIMIT = 56 * 1024 * 1024

QBLK = 256
VROWS = HEAD_DIM + BF16_ROWS
MOE_TM = 1024
NEG = -1e30


def _cparams(sem):
    return pltpu.CompilerParams(dimension_semantics=sem, vmem_limit_bytes=VMEM_LIMIT)


def _mod_rmsnorm(x, g, sc, sh):
    ms = jnp.mean(x * x, axis=-1, keepdims=True)
    y = x * lax.rsqrt(ms + NORM_EPS) * g
    return y * (1.0 + sc) + sh


def _adaln_kernel(c_ref, w_ref, b_ref, o_ref):
    c = c_ref[...]
    ca = c * jax.nn.sigmoid(c)
    o_ref[0] = jnp.dot(ca, w_ref[0], precision=HIGHEST, preferred_element_type=F32) + b_ref[0]


def _adaln(c, w_ada, b_ada):
    n_layers, d, width = w_ada.shape
    b = c.shape[0]
    rows = -(-b // SUBLANE) * SUBLANE
    c_pad = jnp.zeros((rows, d), F32).at[:b].set(c)
    tn = 1024
    out = pl.pallas_call(
        _adaln_kernel,
        out_shape=jax.ShapeDtypeStruct((n_layers, rows, width), F32),
        grid=(n_layers, width // tn),
        in_specs=[
            pl.BlockSpec((rows, d), lambda l, j: (0, 0)),
            pl.BlockSpec((1, d, tn), lambda l, j: (l, 0, j)),
            pl.BlockSpec((1, 1, tn), lambda l, j: (l, 0, j)),
        ],
        out_specs=pl.BlockSpec((1, rows, tn), lambda l, j: (l, 0, j)),
        compiler_params=_cparams(("arbitrary", "arbitrary")),
        name="adaln",
    )(c_pad, w_ada, b_ada.reshape(n_layers, 1, width))
    return out[:, :b].reshape(n_layers, b, N_ADA, d)


def _inproj_kernel(x_ref, g_ref, sc_ref, sh_ref, w_ref, gain_ref, flag_ref, o_ref, h_ref, *, epilogue):
    @pl.when(pl.program_id(1) == 0)
    def _():
        h_ref[...] = _mod_rmsnorm(x_ref[...], g_ref[...], sc_ref[0], sh_ref[0]).astype(BF16)

    y = lax.dot_general(h_ref[...], w_ref[...].astype(BF16), (((1,), (1,)), ((), ())),
                        preferred_element_type=F32)
    if epilogue:
        gain = gain_ref[...]
        flag = flag_ref[...]
        for hh in range(y.shape[1] // HEAD_DIM):
            cs = slice(hh * HEAD_DIM, (hh + 1) * HEAD_DIM)
            yh = y[:, cs]
            ms = jnp.mean(yh * yh, axis=-1, keepdims=True)
            yn = yh * lax.rsqrt(ms + NORM_EPS) * gain[:, cs]
            o_ref[:, cs] = jnp.where(flag[:, cs] > 0.0, yn, yh).astype(o_ref.dtype)
    else:
        o_ref[...] = y.astype(o_ref.dtype)


def _inproj(x2, g, sc, sh, w, gain, flag, *, rows_per_batch, tm, tn, out_dtype, epilogue, name):
    n, d = x2.shape
    width = w.shape[0]
    bpb = rows_per_batch // tm
    return pl.pallas_call(
        functools.partial(_inproj_kernel, epilogue=epilogue),
        out_shape=jax.ShapeDtypeStruct((n, width), out_dtype),
        grid=(n // tm, width // tn),
        in_specs=[
            pl.BlockSpec((tm, d), lambda i, j: (i, 0)),
            pl.BlockSpec((1, d), lambda i, j: (0, 0)),
            pl.BlockSpec((1, 1, d), lambda i, j: (i // bpb, 0, 0)),
            pl.BlockSpec((1, 1, d), lambda i, j: (i // bpb, 0, 0)),
            pl.BlockSpec((tn, d), lambda i, j: (j, 0)),
            pl.BlockSpec((1, tn), lambda i, j: (0, j)),
            pl.BlockSpec((1, tn), lambda i, j: (0, j)),
        ],
        out_specs=pl.BlockSpec((tm, tn), lambda i, j: (i, j)),
        scratch_shapes=[pltpu.VMEM((tm, d), BF16)],
        compiler_params=_cparams(("arbitrary", "arbitrary")),
        name=name,
    )(x2, g, sc, sh, w, gain, flag)


CONV_TT = 256
CONV_HALO = 32
CONV_CHUNK = 32


def _conv_kernel(u_ref, halo_ref, wdw_ref, bdw_ref, lng_ref, lnb_ref, wpw_ref, og_ref, o_ref, z_ref, c_ref):
    t = pl.program_id(1)
    u = u_ref[0]
    z_ref[0, CONV_HALO:, :] = u[:, :CONV_CH] * jax.nn.sigmoid(u[:, CONV_CH:])
    hu = halo_ref[0]
    zh = hu[:, :CONV_CH] * jax.nn.sigmoid(hu[:, CONV_CH:])
    z_ref[0, :CONV_HALO, :] = jnp.where(t == 0, 0.0, zh)
    rows = CONV_TT + CONV_HALO - SUBLANE
    for s in range(1, SUBLANE):
        z_ref[s, :rows, :] = z_ref[0, s:s + rows, :]
    wdw = wdw_ref[...]
    off = CONV_HALO - (CONV_K - 1)
    for c in range(CONV_TT // CONV_CHUNK):
        acc = jnp.zeros((CONV_CHUNK, CONV_CH), F32) + bdw_ref[...]
        for k in range(CONV_K):
            lo = c * CONV_CHUNK + (off + k) // SUBLANE * SUBLANE
            acc = acc + wdw[k:k + 1, :] * z_ref[(off + k) % SUBLANE, lo:lo + CONV_CHUNK, :]
        c_ref[c * CONV_CHUNK:(c + 1) * CONV_CHUNK, :] = acc
    y = c_ref[...]
    mu = jnp.mean(y, axis=-1, keepdims=True)
    var = jnp.mean(jnp.square(y - mu), axis=-1, keepdims=True)
    y = (y - mu) * lax.rsqrt(var + LN_EPS) * lng_ref[...] + lnb_ref[...]
    y = y * jax.nn.sigmoid(y)
    y = jnp.dot(y.astype(BF16), wpw_ref[...], preferred_element_type=F32)
    og = og_ref[...]
    for grp in range(CONV_CH // HEAD_DIM):
        cs = slice(grp * HEAD_DIM, (grp + 1) * HEAD_DIM)
        yh = y[:, cs]
        ms = jnp.mean(yh * yh, axis=-1, keepdims=True)
        o_ref[0, :, cs] = (yh * lax.rsqrt(ms + NORM_EPS) * og[:, cs]).astype(o_ref.dtype)


def _conformer_conv(pa, w_dw, b_dw, ln_g, ln_b, w_pw, out_gain):
    b, t, _ = pa.shape
    hb = CONV_TT // CONV_HALO
    row = lambda v: v.reshape(1, -1)
    vec_spec = pl.BlockSpec((1, CONV_CH), lambda bi, ti: (0, 0))
    return pl.pallas_call(
        _conv_kernel,
        out_shape=jax.ShapeDtypeStruct((b, t, CONV_CH), BF16),
        grid=(b, t // CONV_TT),
        in_specs=[
            pl.BlockSpec((1, CONV_TT, 2 * CONV_CH), lambda bi, ti: (bi, ti, 0)),
            pl.BlockSpec((1, CONV_HALO, 2 * CONV_CH), lambda bi, ti: (bi, jnp.maximum(ti * hb - 1, 0), 0)),
            pl.BlockSpec((CONV_K, CONV_CH), lambda bi, ti: (0, 0)),
            vec_spec, vec_spec, vec_spec,
            pl.BlockSpec((CONV_CH, CONV_CH), lambda bi, ti: (0, 0)),
            vec_spec,
        ],
        out_specs=pl.BlockSpec((1, CONV_TT, CONV_CH), lambda bi, ti: (bi, ti, 0)),
        scratch_shapes=[pltpu.VMEM((SUBLANE, CONV_TT + CONV_HALO, CONV_CH), F32),
                        pltpu.VMEM((CONV_TT, CONV_CH), F32)],
        compiler_params=_cparams(("arbitrary", "arbitrary")),
        name="conformer_conv",
    )(pa, pa, w_dw, row(b_dw), row(ln_g), row(ln_b), w_pw.astype(BF16), row(out_gain))


def _compress_kernel(x_ref, pe_ref, w1_ref, b1_ref, w2_ref, kg_ref, o_ref, acc_ref, *, nj, normalise, transpose_out):
    acc_ref[...] = jnp.zeros_like(acc_ref)
    for c in range(CMP_PER_SEL):
        for pp in range(CMP_STRIDE):
            zc = x_ref[0, pl.ds(CMP_STRIDE * c + pp, nj, stride=SEL_BLK), :]
            lo = (zc + pe_ref[0, pp:pp + 1, :]).astype(BF16)
            hi = (zc + pe_ref[0, CMP_STRIDE + pp:CMP_STRIDE + pp + 1, :]).astype(BF16)
            acc_ref[c] += jnp.dot(lo, w1_ref[0, pp], preferred_element_type=F32)
            acc_ref[CMP_PER_SEL + (c - 1) % CMP_PER_SEL] += jnp.dot(hi, w1_ref[0, CMP_STRIDE + pp],
                                                                  preferred_element_type=F32)
    if transpose_out:
        ones_row = lax.broadcasted_iota(jnp.int32, (BF16_ROWS, CMP_PER_SEL * nj), 0) == 0
        o_ref[0, 0, HEAD_DIM:, :] = ones_row.astype(o_ref.dtype)
    for r in range(CMP_PER_SEL):
        second = acc_ref[CMP_PER_SEL + r]
        if r == CMP_PER_SEL - 1:
            second = pltpu.roll(second, nj - 1, 0)
        act = jax.nn.gelu(acc_ref[r] + second + b1_ref[0])
        if transpose_out:
            out = jnp.dot(w2_ref[0], act.T.astype(BF16), preferred_element_type=F32)
            o_ref[0, 0, :HEAD_DIM, r * nj:(r + 1) * nj] = out.astype(o_ref.dtype)
        else:
            out = jnp.dot(act.astype(BF16), w2_ref[0], preferred_element_type=F32)
            if normalise:
                ms = jnp.mean(out * out, axis=-1, keepdims=True)
                out = out * lax.rsqrt(ms + NORM_EPS) * kg_ref[...]
            o_ref[0, 0, r * nj:(r + 1) * nj, :] = out.astype(o_ref.dtype)


def _compress(pa, col0, pe, w1, b1, w2, kgain, *, normalise, transpose_out):
    b, t, _ = pa.shape
    nj = t // SEL_BLK
    ncp = CMP_PER_SEL * nj
    cb0 = col0 // HEAD_DIM
    oshape = (b, NSA_KV, VROWS, ncp) if transpose_out else (b, NSA_KV, ncp, HEAD_DIM)
    assert not (normalise and transpose_out)
    w2_in = (w2.T if transpose_out else w2)[None].astype(BF16)
    return pl.pallas_call(
        functools.partial(_compress_kernel, nj=nj, normalise=normalise, transpose_out=transpose_out),
        out_shape=jax.ShapeDtypeStruct(oshape, BF16),
        grid=(b, NSA_KV),
        in_specs=[
            pl.BlockSpec((1, t, HEAD_DIM), lambda bi, g: (bi, 0, cb0 + g)),
            pl.BlockSpec((1, CMP_BLK, HEAD_DIM), lambda bi, g: (0, 0, 0)),
            pl.BlockSpec((1, CMP_BLK, HEAD_DIM, CMP_HIDDEN), lambda bi, g: (0, 0, 0, 0)),
            pl.BlockSpec((1, 1, CMP_HIDDEN), lambda bi, g: (0, 0, 0)),
            pl.BlockSpec((1,) + w2_in.shape[1:], lambda bi, g: (0, 0, 0)),
            pl.BlockSpec((1, HEAD_DIM), lambda bi, g: (0, 0)),
        ],
        out_specs=pl.BlockSpec((1, 1) + oshape[2:], lambda bi, g: (bi, g, 0, 0)),
        scratch_shapes=[pltpu.VMEM((2 * CMP_PER_SEL, nj, CMP_HIDDEN), F32)],
        compiler_params=_cparams(("arbitrary", "arbitrary")),
        name="nsa_compress_v" if transpose_out else "nsa_compress_k",
    )(pa, pe[None], w1.reshape(1, CMP_BLK, HEAD_DIM, CMP_HIDDEN).astype(BF16), b1.reshape(1, 1, -1),
      w2_in, kgain.reshape(1, -1))


KVP_TT = 512


def _kv_prep_kernel(k_ref, v_ref, ka_ref, vt_ref, *, augment_keys):
    tt = v_ref.shape[1]
    vt_ref[0, 0, :HEAD_DIM, :] = v_ref[0].astype(F32).T.astype(vt_ref.dtype)
    vt_ref[0, 0, HEAD_DIM:, :] = (lax.broadcasted_iota(jnp.int32, (BF16_ROWS, tt), 0) == 0).astype(vt_ref.dtype)
    if augment_keys:
        key = pl.program_id(2) * tt + lax.broadcasted_iota(jnp.int32, (tt, 1), 0)
        lane = lax.broadcasted_iota(jnp.int32, (tt, LANE), 1)
        ka_ref[0, 0, :, :HEAD_DIM] = k_ref[0]
        ka_ref[0, 0, :, HEAD_DIM:] = (lane == (key // SEL_BLK) % LANE).astype(ka_ref.dtype)
    else:
        ka_ref[...] = jnp.zeros_like(ka_ref)


def _kv_prep(pb, k_cb, v_cb, *, augment_keys):
    b, t, _ = pb.shape
    ka_shape = (b, NSA_KV, t, 2 * HEAD_DIM) if augment_keys else (b, NSA_KV, BF16_ROWS, LANE)
    ka_block = (1, 1, KVP_TT, 2 * HEAD_DIM) if augment_keys else (1, 1, BF16_ROWS, LANE)
    ka_map = (lambda bi, g, ti: (bi, g, ti, 0)) if augment_keys else (lambda bi, g, ti: (bi, g, 0, 0))
    return pl.pallas_call(
        functools.partial(_kv_prep_kernel, augment_keys=augment_keys),
        out_shape=(jax.ShapeDtypeStruct(ka_shape, BF16),
                   jax.ShapeDtypeStruct((b, NSA_KV, VROWS, t), BF16)),
        grid=(b, NSA_KV, t // KVP_TT),
        in_specs=[pl.BlockSpec((1, KVP_TT, HEAD_DIM), lambda bi, g, ti: (bi, ti, k_cb + g)),
                  pl.BlockSpec((1, KVP_TT, HEAD_DIM), lambda bi, g, ti: (bi, ti, v_cb + g))],
        out_specs=(pl.BlockSpec(ka_block, ka_map),
                   pl.BlockSpec((1, 1, VROWS, KVP_TT), lambda bi, g, ti: (bi, g, 0, ti))),
        compiler_params=_cparams(("arbitrary", "arbitrary", "arbitrary")),
        name="kv_prep_aug" if augment_keys else "kv_prep",
    )(pb, pb)


def _query_positions(start, nq):
    col = lax.broadcasted_iota(jnp.int32, (1, NSA_HPG * nq), 1)
    return start + col % nq


def _transpose_queries(q):
    qf = q.astype(F32)
    return jnp.concatenate([qf[:, h * HEAD_DIM:(h + 1) * HEAD_DIM].T for h in range(NSA_HPG)], axis=1)


def _normed_heads_out(o_ref, y, gain, nq):
    ms = jnp.mean(y * y, axis=0, keepdims=True)
    y = y * lax.rsqrt(ms + NORM_EPS) * gain
    for h in range(NSA_HPG):
        o_ref[0, :, h * HEAD_DIM:(h + 1) * HEAD_DIM] = y[:, h * nq:(h + 1) * nq].T.astype(o_ref.dtype)


def _nsa_cmp_kernel(q_ref, kc_ref, vct_ref, o_ref, qs_ref, *, nq, nj, nselp, n_top):
    start = pl.program_id(2) * nq
    qt = _transpose_queries(q_ref[0]).astype(BF16)
    s = jnp.dot(kc_ref[0, 0], qt, preferred_element_type=F32)
    ncp = CMP_PER_SEL * nj
    row = lax.broadcasted_iota(jnp.int32, (ncp, 1), 0)
    cmp_end = SEL_BLK * (row % nj) + CMP_STRIDE * (row // nj) + (CMP_BLK - 1)
    tpos = _query_positions(start, nq)
    s = jnp.where(cmp_end <= tpos, s, NEG)
    m = jnp.max(s, axis=0, keepdims=True)
    m = jnp.where(m <= 0.5 * NEG, 0.0, m)
    p = jnp.exp2(s - m)
    pv = jnp.dot(vct_ref[0, 0], p.astype(BF16), preferred_element_type=F32)
    inv = 1.0 / jnp.maximum(pv[HEAD_DIM:HEAD_DIM + 1, :], 1e-30)
    o_ref[0, 0, 0] = pv[:HEAD_DIM, :] * inv

    imp = p[:, 0:nq] * inv[:, 0:nq]
    for h in range(1, NSA_HPG):
        imp = imp + p[:, h * nq:(h + 1) * nq] * inv[:, h * nq:(h + 1) * nq]
    last = imp[(CMP_PER_SEL - 1) * nj:, :]
    jrow = lax.broadcasted_iota(jnp.int32, (nj, 1), 0)
    prev = jnp.where(jrow == 0, 0.0, pltpu.roll(last, 1, 0))
    imp_sel = prev + last
    for r in range(CMP_PER_SEL - 1):
        imp_sel = imp_sel + imp[r * nj:(r + 1) * nj, :]

    cur = tpos[:, :nq] // SEL_BLK
    behind = lax.bitcast_convert_type(cur - jrow, jnp.uint32)
    selneg = jnp.where(behind <= 1, 0.0, NEG)
    selneg = jnp.where(jrow == 0, 0.0, selneg)
    score = jnp.where(jrow <= cur, imp_sel, NEG)
    score = jnp.where(selneg == 0.0, 2.0 * NEG, score)
    for _ in range(max(n_top - 3, 0)):
        mx = jnp.max(score, axis=0, keepdims=True)
        idx = jnp.min(jnp.where(score == mx, jrow, nj), axis=0, keepdims=True)
        idx = jnp.where(mx > 0.5 * NEG, idx, -1)
        hit = jrow == idx
        selneg = jnp.where(hit, 0.0, selneg)
        score = jnp.where(hit, 2.0 * NEG, score)
    if nselp > nj:
        selneg = jnp.concatenate([selneg, jnp.full((nselp - nj, nq), NEG, F32)], axis=0)
    selneg = jnp.concatenate([selneg] * NSA_HPG, axis=1).astype(BF16)
    for hf in range(nselp // LANE):
        qs_ref[0, 0, 0, 2 * LANE * hf:2 * LANE * hf + LANE, :] = qt
        qs_ref[0, 0, 0, 2 * LANE * hf + LANE:2 * LANE * (hf + 1), :] = selneg[hf * LANE:(hf + 1) * LANE, :]


def _nsa_cmp(pb, kc, vct, t):
    b = pb.shape[0]
    nq = QBLK
    nqb = t // nq
    nj = t // SEL_BLK
    nselp = -(-nj // LANE) * LANE
    n_top = min(SEL_TOP, nj)
    ncp = CMP_PER_SEL * nj
    return pl.pallas_call(
        functools.partial(_nsa_cmp_kernel, nq=nq, nj=nj, nselp=nselp, n_top=n_top),
        out_shape=(jax.ShapeDtypeStruct((b, NSA_KV, nqb, HEAD_DIM, NSA_HPG * nq), F32),
                   jax.ShapeDtypeStruct((b, NSA_KV, nqb, 2 * nselp, NSA_HPG * nq), BF16)),
        grid=(b, NSA_KV, nqb),
        in_specs=[
            pl.BlockSpec((1, nq, NSA_HPG * HEAD_DIM), lambda bi, g, qi: (bi, qi, g)),
            pl.BlockSpec((1, 1, ncp, HEAD_DIM), lambda bi, g, qi: (bi, g, 0, 0)),
            pl.BlockSpec((1, 1, VROWS, ncp), lambda bi, g, qi: (bi, g, 0, 0)),
        ],
        out_specs=(pl.BlockSpec((1, 1, 1, HEAD_DIM, NSA_HPG * nq), lambda bi, g, qi: (bi, g, qi, 0, 0)),
                   pl.BlockSpec((1, 1, 1, 2 * nselp, NSA_HPG * nq), lambda bi, g, qi: (bi, g, qi, 0, 0))),
        compiler_params=_cparams(("arbitrary", "arbitrary", "arbitrary")),
        name="nsa_compressed_topk",
    )(pb, kc, vct)


SLC_QB = 2
KVT = 256


def _nsa_slc_kernel(qs_ref, ka_ref, vt_ref, o_ref, acc_ref, sa_ref, sb_ref, p_ref, m_ref, *, nq):
    qi = pl.program_id(2)
    span = SLC_QB * nq
    start = qi * span
    bcol = NSA_HPG * nq
    ncol = SLC_QB * bcol
    tiles_per_half = LANE * SEL_BLK // KVT
    n_diag = span // KVT
    n_past = qi * n_diag
    krow = lax.broadcasted_iota(jnp.int32, (KVT, 1), 0)
    lane = lax.broadcasted_iota(jnp.int32, (1, LANE), 1)
    acc_ref[...] = jnp.zeros_like(acc_ref)
    m_ref[...] = jnp.full_like(m_ref, NEG)

    def scores(ti, dst_ref):
        k0 = pl.multiple_of(ti * KVT, KVT)
        r0 = pl.multiple_of((ti // tiles_per_half) * 2 * LANE, 2 * LANE)
        ka = ka_ref[0, 0, pl.ds(k0, KVT), :]
        for qb in range(SLC_QB):
            dst_ref[:, qb * bcol:(qb + 1) * bcol] = jnp.dot(ka, qs_ref[0, 0, qb, pl.ds(r0, 2 * LANE), :],
                                                            preferred_element_type=F32)

    def softmax_pv(src_ref, ti, diagonal):
        k0 = pl.multiple_of(ti * KVT, KVT)
        for c in range(ncol // LANE):
            cs = slice(c * LANE, (c + 1) * LANE)
            s = src_ref[:, cs]
            if diagonal:
                q0 = (c * LANE // bcol) * nq + (c * LANE) % nq
                s = jnp.where(k0 + krow <= start + q0 + lane, s, NEG)
            m = m_ref[:, cs]
            mn = jnp.maximum(m, jnp.max(s, axis=0, keepdims=True))
            p_ref[:, cs] = jnp.exp2(s - mn).astype(BF16)
            m_ref[:, cs] = mn
            acc_ref[:, cs] = acc_ref[:, cs] * jnp.exp2(m - mn)
        acc_ref[...] += jnp.dot(vt_ref[0, 0, :, pl.ds(k0, KVT)], p_ref[...], preferred_element_type=F32)

    scores(0, sa_ref)

    def pair(pi, carry):
        t0 = 2 * pi
        scores(t0 + 1, sb_ref)
        softmax_pv(sa_ref, t0, False)
        scores(t0 + 2, sa_ref)
        softmax_pv(sb_ref, t0 + 1, False)
        return carry

    lax.fori_loop(0, n_past // 2, pair, 0)
    bufs = (sa_ref, sb_ref)
    for dt in range(n_diag):
        if dt + 1 < n_diag:
            scores(n_past + dt + 1, bufs[(dt + 1) % 2])
        softmax_pv(bufs[dt % 2], n_past + dt, True)

    for qb in range(SLC_QB):
        cs = slice(qb * bcol, (qb + 1) * bcol)
        o_ref[0, 0, qb] = acc_ref[:HEAD_DIM, cs] / acc_ref[HEAD_DIM:HEAD_DIM + 1, cs]


def _nsa_slc(qsel, ka, vt, t):
    b = qsel.shape[0]
    nq = QBLK
    nqb = t // nq
    rows = qsel.shape[3]
    bcol = NSA_HPG * nq
    ncol = SLC_QB * bcol
    assert nqb % SLC_QB == 0 and (SLC_QB * nq // KVT) % 2 == 0
    return pl.pallas_call(
        functools.partial(_nsa_slc_kernel, nq=nq),
        out_shape=jax.ShapeDtypeStruct((b, NSA_KV, nqb, HEAD_DIM, bcol), F32),
        grid=(b, NSA_KV, nqb // SLC_QB),
        in_specs=[
            pl.BlockSpec((1, 1, SLC_QB, rows, bcol), lambda bi, g, qi: (bi, g, qi, 0, 0)),
            pl.BlockSpec((1, 1, t, 2 * HEAD_DIM), lambda bi, g, qi: (bi, g, 0, 0)),
            pl.BlockSpec((1, 1, VROWS, t), lambda bi, g, qi: (bi, g, 0, 0)),
        ],
        out_specs=pl.BlockSpec((1, 1, SLC_QB, HEAD_DIM, bcol), lambda bi, g, qi: (bi, g, qi, 0, 0)),
        scratch_shapes=[pltpu.VMEM((VROWS, ncol), F32), pltpu.VMEM((KVT, ncol), F32),
                        pltpu.VMEM((KVT, ncol), F32), pltpu.VMEM((KVT, ncol), BF16),
                        pltpu.VMEM((1, ncol), F32)],
        compiler_params=_cparams(("arbitrary", "arbitrary", "arbitrary")),
        name="nsa_selected",
    )(qsel, ka, vt)


def _band_scores(qt, k_ref, vt_ref, start, nq, nk, window, t_len, sink):
    k0 = pl.multiple_of(jnp.clip(start + nq - nk, 0, t_len - nk), LANE)
    s = jnp.dot(k_ref[0, pl.ds(k0, nk), :], qt, preferred_element_type=F32)
    rel = _query_positions(start, nq) - (k0 + lax.broadcasted_iota(jnp.int32, (nk, 1), 0))
    s = jnp.where(lax.bitcast_convert_type(rel, jnp.uint32) < window, s, NEG)
    m = jnp.max(s, axis=0, keepdims=True)
    if sink is not None:
        m = jnp.maximum(m, sink)
    p = jnp.exp2(s - m).astype(BF16)
    pv = jnp.dot(vt_ref[0, 0, :, pl.ds(k0, nk)], p, preferred_element_type=F32)
    den = pv[HEAD_DIM:HEAD_DIM + 1, :]
    if sink is not None:
        den = den + jnp.exp2(sink - m)
    return pv[:HEAD_DIM, :] / den


def _band_keys(window):
    return -(-(window - 1) // LANE) * LANE + QBLK


def _nsa_finish_kernel(qs_ref, k_ref, vt_ref, oc_ref, os_ref, gl_ref, gain_ref, o_ref, gt_ref, *, nq, t_len):
    g = pl.program_id(1)
    start = pl.program_id(2) * nq
    o_win = _band_scores(qs_ref[0, 0, 0], k_ref, vt_ref, start, nq, _band_keys(NSA_WIN), NSA_WIN, t_len, None)
    gt_ref[...] = jax.nn.sigmoid(gl_ref[...]).T
    branch = []
    for br in range(3):
        rows = [gt_ref[pl.ds((g * NSA_HPG + h) * 3 + br, 1), :] for h in range(NSA_HPG)]
        branch.append(jnp.concatenate(rows, axis=1))
    y = branch[0] * oc_ref[0, 0, 0] + branch[1] * os_ref[0, 0, 0] + branch[2] * o_win
    _normed_heads_out(o_ref, y, gain_ref[0], nq)


def _nsa_finish(qsel, pb, k_cb, vt, o_cmp, o_slc, pa2, gate_cb, gain_t, t):
    b = pb.shape[0]
    nq = QBLK
    nqb = t // nq
    ncol = NSA_HPG * nq
    blk5 = pl.BlockSpec((1, 1, 1, HEAD_DIM, ncol), lambda bi, g, qi: (bi, g, qi, 0, 0))
    return pl.pallas_call(
        functools.partial(_nsa_finish_kernel, nq=nq, t_len=t),
        out_shape=jax.ShapeDtypeStruct((b, t, NSA_HEADS * HEAD_DIM), BF16),
        grid=(b, NSA_KV, nqb),
        in_specs=[
            blk5,
            pl.BlockSpec((1, t, HEAD_DIM), lambda bi, g, qi: (bi, 0, k_cb + g)),
            pl.BlockSpec((1, 1, VROWS, t), lambda bi, g, qi: (bi, g, 0, 0)),
            blk5, blk5,
            pl.BlockSpec((nq, LANE), lambda bi, g, qi: (bi * nqb + qi, gate_cb)),
            pl.BlockSpec((1, HEAD_DIM, ncol), lambda bi, g, qi: (g, 0, 0)),
        ],
        out_specs=pl.BlockSpec((1, nq, NSA_HPG * HEAD_DIM), lambda bi, g, qi: (bi, qi, g)),
        scratch_shapes=[pltpu.VMEM((LANE, nq), F32)],
        compiler_params=_cparams(("arbitrary", "arbitrary", "arbitrary")),
        name="nsa_window_finish",
    )(qsel, pb, vt, o_cmp, o_slc, pa2, gain_t)


def _swa_kernel(q_ref, k_ref, vt_ref, sink_ref, gain_ref, o_ref, *, nq, t_len):
    start = pl.program_id(2) * nq
    qt = _transpose_queries(q_ref[0]).astype(BF16)
    y = _band_scores(qt, k_ref, vt_ref, start, nq, _band_keys(SWA_WIN), SWA_WIN, t_len, sink_ref[0])
    _normed_heads_out(o_ref, y, gain_ref[0], nq)


def _swa(pb, q_cb, k_cb, vt, sink_rows, gain_t, t):
    b = pb.shape[0]
    nq = QBLK
    nqb = t // nq
    ncol = SWA_HPG * nq
    return pl.pallas_call(
        functools.partial(_swa_kernel, nq=nq, t_len=t),
        out_shape=jax.ShapeDtypeStruct((b, t, SWA_HEADS * HEAD_DIM), BF16),
        grid=(b, SWA_KV, nqb),
        in_specs=[
            pl.BlockSpec((1, nq, SWA_HPG * HEAD_DIM), lambda bi, g, qi: (bi, qi, q_cb + g)),
            pl.BlockSpec((1, t, HEAD_DIM), lambda bi, g, qi: (bi, 0, k_cb + g)),
            pl.BlockSpec((1, 1, VROWS, t), lambda bi, g, qi: (bi, g, 0, 0)),
            pl.BlockSpec((1, 1, ncol), lambda bi, g, qi: (g, 0, 0)),
            pl.BlockSpec((1, HEAD_DIM, ncol), lambda bi, g, qi: (g, 0, 0)),
        ],
        out_specs=pl.BlockSpec((1, nq, SWA_HPG * HEAD_DIM), lambda bi, g, qi: (bi, qi, g)),
        compiler_params=_cparams(("arbitrary", "arbitrary", "arbitrary")),
        name="swa_sink",
    )(pb, pb, vt, sink_rows, gain_t)


def _outproj_kernel(ya_ref, yb_ref, yc_ref, wa_ref, wb_ref, wc_ref, x_ref, gm_ref, o_ref):
    y = jnp.dot(ya_ref[...], wa_ref[...], preferred_element_type=F32)
    y = y + jnp.dot(yb_ref[...], wb_ref[...], preferred_element_type=F32)
    y = y + jnp.dot(yc_ref[...], wc_ref[...], preferred_element_type=F32)
    o_ref[...] = x_ref[...] + gm_ref[0] * y


def _outproj(ya, yb, yc, w_out, x2, gm, *, rows_per_batch):
    n, d = x2.shape
    tm, tn = 1024, 512
    bpb = rows_per_batch // tm
    wa, wb, wc = ya.shape[1], yb.shape[1], yc.shape[1]
    w = w_out.astype(BF16)
    return pl.pallas_call(
        _outproj_kernel,
        out_shape=jax.ShapeDtypeStruct((n, d), F32),
        grid=(n // tm, d // tn),
        in_specs=[
            pl.BlockSpec((tm, wa), lambda i, j: (i, 0)),
            pl.BlockSpec((tm, wb), lambda i, j: (i, 0)),
            pl.BlockSpec((tm, wc), lambda i, j: (i, 0)),
            pl.BlockSpec((wa, tn), lambda i, j: (0, j)),
            pl.BlockSpec((wb, tn), lambda i, j: (0, j)),
            pl.BlockSpec((wc, tn), lambda i, j: (0, j)),
            pl.BlockSpec((tm, tn), lambda i, j: (i, j)),
            pl.BlockSpec((1, 1, tn), lambda i, j: (i // bpb, 0, j)),
        ],
        out_specs=pl.BlockSpec((tm, tn), lambda i, j: (i, j)),
        compiler_params=_cparams(("arbitrary", "arbitrary")),
        name="mix_outproj",
    )(ya, yb, yc, w[:wa], w[wa:wa + wb], w[wa + wb:], x2, gm)


def _ffn_kernel(x_ref, g_ref, sc_ref, sh_ref, gf_ref, wg_ref, wu_ref, wd_ref, o_ref, h_ref):
    f = pl.program_id(1)

    @pl.when(f == 0)
    def _():
        h_ref[...] = _mod_rmsnorm(x_ref[...], g_ref[...], sc_ref[0], sh_ref[0]).astype(BF16)
        o_ref[...] = jnp.zeros_like(o_ref)

    h = h_ref[...]
    a = jnp.dot(h, wg_ref[...], preferred_element_type=F32)
    u = jnp.dot(h, wu_ref[...], preferred_element_type=F32)
    act = (a * jax.nn.sigmoid(a) * u).astype(BF16)
    o_ref[...] += jnp.dot(act, wd_ref[...], preferred_element_type=F32)

    @pl.when(f == pl.num_programs(1) - 1)
    def _():
        o_ref[...] = x_ref[...] + gf_ref[0] * o_ref[...]


def _ffn_dense(x2, g, sc, sh, gf, wg, wu, wd, *, rows_per_batch):
    n, d = x2.shape
    ff = wg.shape[1]
    tm, tf = 512, 512
    bpb = rows_per_batch // tm
    mod_spec = pl.BlockSpec((1, 1, d), lambda i, f: (i // bpb, 0, 0))
    return pl.pallas_call(
        _ffn_kernel,
        out_shape=jax.ShapeDtypeStruct((n, d), F32),
        grid=(n // tm, ff // tf),
        in_specs=[
            pl.BlockSpec((tm, d), lambda i, f: (i, 0)),
            pl.BlockSpec((1, d), lambda i, f: (0, 0)),
            mod_spec, mod_spec, mod_spec,
            pl.BlockSpec((d, tf), lambda i, f: (0, f)),
            pl.BlockSpec((d, tf), lambda i, f: (0, f)),
            pl.BlockSpec((tf, d), lambda i, f: (f, 0)),
        ],
        out_specs=pl.BlockSpec((tm, d), lambda i, f: (i, 0)),
        scratch_shapes=[pltpu.VMEM((tm, d), BF16)],
        compiler_params=_cparams(("arbitrary", "arbitrary")),
        name="ffn_dense",
    )(x2, g, sc, sh, gf, wg, wu, wd)


def _router_kernel(x_ref, g_ref, sc_ref, sh_ref, wr_ref, h_ref, idx_ref, wt_ref):
    h = _mod_rmsnorm(x_ref[...], g_ref[...], sc_ref[0], sh_ref[0])
    h_ref[...] = h
    logits = jnp.dot(h, wr_ref[...], precision=HIGHEST, preferred_element_type=F32)
    e = lax.broadcasted_iota(jnp.int32, logits.shape, 1)
    m1 = jnp.max(logits, axis=-1, keepdims=True)
    i1 = jnp.min(jnp.where(logits == m1, e, N_EXPERTS), axis=-1, keepdims=True)
    rest = jnp.where(e == i1, -jnp.inf, logits)
    m2 = jnp.max(rest, axis=-1, keepdims=True)
    i2 = jnp.min(jnp.where(rest == m2, e, N_EXPERTS), axis=-1, keepdims=True)
    e2 = jnp.exp(m2 - m1)
    den = 1.0 + e2
    idx_ref[:, 0:1] = i1
    idx_ref[:, 1:2] = i2
    wt_ref[:, 0:1] = 1.0 / den
    wt_ref[:, 1:2] = e2 / den


def _router(x2, g, sc, sh, w_router, *, rows_per_batch):
    n, d = x2.shape
    tm = 512
    bpb = rows_per_batch // tm
    mod_spec = pl.BlockSpec((1, 1, d), lambda i: (i // bpb, 0, 0))
    return pl.pallas_call(
        _router_kernel,
        out_shape=(jax.ShapeDtypeStruct((n, d), F32),
                   jax.ShapeDtypeStruct((n, TOP_K), jnp.int32),
                   jax.ShapeDtypeStruct((n, TOP_K), F32)),
        grid=(n // tm,),
        in_specs=[
            pl.BlockSpec((tm, d), lambda i: (i, 0)),
            pl.BlockSpec((1, d), lambda i: (0, 0)),
            mod_spec, mod_spec,
            pl.BlockSpec((d, N_EXPERTS), lambda i: (0, 0)),
        ],
        out_specs=(pl.BlockSpec((tm, d), lambda i: (i, 0)),
                   pl.BlockSpec((tm, TOP_K), lambda i: (i, 0)),
                   pl.BlockSpec((tm, TOP_K), lambda i: (i, 0))),
        compiler_params=_cparams(("arbitrary",)),
        name="moe_router",
    )(x2, g, sc, sh, w_router)


DMA_UNROLL = 8


def _prefetch_rows(n_f):
    step = SUBLANE // math.gcd(SUBLANE, n_f)
    return -(-(-(-MOE_TM // n_f)) // step) * step


def _expert_kernel(be_ref, src_ref, nused_ref, h_hbm, wg_ref, wu_ref, wd_ref, o_ref, xg_ref, xb_ref, sem,
                   *, n_blocks, n_f):
    i = pl.program_id(0)
    f = pl.program_id(1)
    used = i < nused_ref[0]
    pf_rows = _prefetch_rows(n_f)
    pf_total = pf_rows * n_f

    def row_copy(blk, r):
        src = src_ref[blk * MOE_TM + jnp.minimum(r, MOE_TM - 1)]
        return pltpu.make_async_copy(h_hbm.at[pl.ds(src, 1), :], xg_ref.at[pl.ds(r, 1), :], sem)

    def wait_gather():
        pltpu.make_async_copy(h_hbm.at[pl.ds(0, pf_total), :], xg_ref.at[pl.ds(0, pf_total), :], sem).wait()

    def prefetch_chunk():
        nxt = jnp.minimum(i + 1, n_blocks - 1)
        for j in range(pf_rows):
            row_copy(nxt, f * pf_rows + j).start()

    @pl.when((i == 0) & (f == 0))
    def _():
        lax.fori_loop(0, pf_total, lambda r, c: (row_copy(0, r).start(), c)[1], 0, unroll=DMA_UNROLL)

    @pl.when(f == 0)
    def _():
        wait_gather()
        xb_ref[...] = xg_ref[:MOE_TM, :].astype(BF16)
        o_ref[...] = jnp.zeros_like(o_ref)

    @pl.when(used)
    def _():
        prefetch_chunk()
        xb = xb_ref[...]
        a = jnp.dot(xb, wg_ref[0], preferred_element_type=F32)
        u = jnp.dot(xb, wu_ref[0], preferred_element_type=F32)
        act = (a * jax.nn.sigmoid(a) * u).astype(BF16)
        o_ref[...] += jnp.dot(act, wd_ref[0], preferred_element_type=F32)

    @pl.when(jnp.logical_not(used))
    def _():
        prefetch_chunk()

    @pl.when((i == n_blocks - 1) & (f == n_f - 1))
    def _():
        wait_gather()


def _experts(h_tok, block_expert, src_row, n_used, wg, wu, wd, n_blocks):
    n, d = h_tok.shape
    ff = wg.shape[2]
    tf = 512
    n_f = ff // tf
    xg_rows = _prefetch_rows(n_f) * n_f
    return pl.pallas_call(
        functools.partial(_expert_kernel, n_blocks=n_blocks, n_f=n_f),
        out_shape=jax.ShapeDtypeStruct((n_blocks * MOE_TM, d), F32),
        grid_spec=pltpu.PrefetchScalarGridSpec(
            num_scalar_prefetch=3,
            grid=(n_blocks, n_f),
            in_specs=[
                pl.BlockSpec(memory_space=pl.ANY),
                pl.BlockSpec((1, d, tf), lambda i, f, be, src, nu: (be[i], 0, f)),
                pl.BlockSpec((1, d, tf), lambda i, f, be, src, nu: (be[i], 0, f)),
                pl.BlockSpec((1, tf, d), lambda i, f, be, src, nu: (be[i], f, 0)),
            ],
            out_specs=pl.BlockSpec((MOE_TM, d), lambda i, f, be, src, nu: (i, 0)),
            scratch_shapes=[pltpu.VMEM((xg_rows, d), F32), pltpu.VMEM((MOE_TM, d), BF16),
                            pltpu.SemaphoreType.DMA],
        ),
        compiler_params=_cparams(("arbitrary", "arbitrary")),
        name="moe_experts",
    )(block_expert, src_row, n_used, h_tok, wg, wu, wd)


CMB_TM = 256


def _combine_kernel(pos_ref, y_hbm, x_ref, wt_ref, gf_ref, o_ref, yb_ref, sem):
    i = pl.program_id(0)

    def row_copy(r, k):
        return pltpu.make_async_copy(y_hbm.at[pl.ds(pos_ref[(i * CMB_TM + r) * TOP_K + k], 1), :],
                                     yb_ref.at[k, pl.ds(r, 1), :], sem)

    def start(r, c):
        for k in range(TOP_K):
            row_copy(r, k).start()
        return c

    def wait(r, c):
        for k in range(TOP_K):
            row_copy(r, k).wait()
        return c

    lax.fori_loop(0, CMB_TM, start, 0, unroll=DMA_UNROLL)
    lax.fori_loop(0, CMB_TM, wait, 0, unroll=DMA_UNROLL)
    wt = wt_ref[...]
    f = yb_ref[0] * wt[:, 0:1]
    for k in range(1, TOP_K):
        f = f + yb_ref[k] * wt[:, k:k + 1]
    o_ref[...] = x_ref[...] + gf_ref[0] * f


def _combine(pos, y_buf, x2, wt, gf, *, rows_per_batch):
    n, d = x2.shape
    bpb = rows_per_batch // CMB_TM
    return pl.pallas_call(
        _combine_kernel,
        out_shape=jax.ShapeDtypeStruct((n, d), F32),
        grid_spec=pltpu.PrefetchScalarGridSpec(
            num_scalar_prefetch=1,
            grid=(n // CMB_TM,),
            in_specs=[
                pl.BlockSpec(memory_space=pl.ANY),
                pl.BlockSpec((CMB_TM, d), lambda i, pos: (i, 0)),
                pl.BlockSpec((CMB_TM, TOP_K), lambda i, pos: (i, 0)),
                pl.BlockSpec((1, 1, d), lambda i, pos: (i // bpb, 0, 0)),
            ],
            out_specs=pl.BlockSpec((CMB_TM, d), lambda i, pos: (i, 0)),
            scratch_shapes=[pltpu.VMEM((TOP_K, CMB_TM, d), F32), pltpu.SemaphoreType.DMA],
        ),
        compiler_params=_cparams(("arbitrary",)),
        name="moe_combine",
    )(pos, y_buf, x2, wt, gf)


def _moe(x2, g, sc, sh, gf, w_router, wg, wu, wd, *, rows_per_batch):
    n, d = x2.shape
    h_tok, top_idx, top_w = _router(x2, g, sc, sh, w_router, rows_per_batch=rows_per_batch)
    n_assign = n * TOP_K
    n_blocks = n_assign // MOE_TM + N_EXPERTS
    expert_of = top_idx.reshape(-1)
    onehot = (expert_of[:, None] == jnp.arange(N_EXPERTS)[None, :]).astype(jnp.int32)
    rank = jnp.take_along_axis(jnp.cumsum(onehot, axis=0) - onehot, expert_of[:, None], axis=1)[:, 0]
    counts = jnp.sum(onehot, axis=0)
    padded = (counts + MOE_TM - 1) // MOE_TM * MOE_TM
    pad_ends = jnp.cumsum(padded)
    pad_starts = pad_ends - padded
    dest = pad_starts[expert_of] + rank
    token_of = jnp.arange(n_assign, dtype=jnp.int32) // TOP_K
    src_row = jnp.zeros((n_blocks * MOE_TM,), jnp.int32).at[dest].set(token_of)
    block_start = jnp.arange(n_blocks, dtype=jnp.int32) * MOE_TM
    block_expert = jnp.minimum(jnp.sum(block_start[:, None] >= pad_ends[None, :], axis=1),
                               N_EXPERTS - 1).astype(jnp.int32)
    n_used = (pad_ends[-1] // MOE_TM).astype(jnp.int32).reshape(1)
    y_buf = _experts(h_tok, block_expert, src_row, n_used, wg, wu, wd, n_blocks)
    return _combine(dest.astype(jnp.int32), y_buf, x2, top_w, gf, rows_per_batch=rows_per_batch)


PA_KC = 2 * CONV_CH
PA_VC = PA_KC + NSA_KV * HEAD_DIM
PA_GATE = PA_VC + NSA_KV * HEAD_DIM
PA_WIDTH = PA_GATE + LANE
PB_QN = 0
PB_QS = NSA_HEADS * HEAD_DIM
PB_KS = PB_QS + SWA_HEADS * HEAD_DIM
PB_VS = PB_KS + NSA_KV * HEAD_DIM
PB_KW = PB_VS + NSA_KV * HEAD_DIM
PB_VW = PB_KW + NSA_KV * HEAD_DIM
PB_KC = PB_VW + NSA_KV * HEAD_DIM
PB_VC = PB_KC + SWA_KV * HEAD_DIM
PB_WIDTH = PB_VC + SWA_KV * HEAD_DIM


def _split_w_in(w_in, nsa_q_g, nsa_k_g, swa_q_g, swa_k_g):
    sizes = [2 * CONV_CH, NSA_HEADS * HEAD_DIM] + [NSA_KV * HEAD_DIM] * 6 + [
        3 * NSA_HEADS, SWA_HEADS * HEAD_DIM, SWA_KV * HEAD_DIM, SWA_KV * HEAD_DIM]
    offs = [0]
    for s in sizes:
        offs.append(offs[-1] + s)
    w_t = w_in.T
    seg = [w_t[offs[i]:offs[i + 1]] for i in range(len(sizes))]
    (u_conv, q_n, kc, vc, ks, vs, kw, vw, g_n, q_s, k_s, v_s) = seg
    d = w_in.shape[0]
    g_pad = jnp.zeros((LANE - 3 * NSA_HEADS, d), w_in.dtype)
    wa = jnp.concatenate([u_conv, kc, vc, g_n, g_pad], axis=0).astype(BF16)
    wb = jnp.concatenate([q_n, q_s, ks, vs, kw, vw, k_s, v_s], axis=0)
    qscale = HEAD_DIM ** -0.5 * LOG2E
    ones = jnp.ones((HEAD_DIM,), F32)

    def rep(v, n):
        return jnp.tile(v.astype(F32), n)

    gain = jnp.concatenate([rep(nsa_q_g * qscale, NSA_HEADS), rep(swa_q_g * qscale, SWA_HEADS),
                            rep(nsa_k_g[1], NSA_KV), rep(ones, NSA_KV), rep(nsa_k_g[2], NSA_KV),
                            rep(ones, NSA_KV), rep(swa_k_g, SWA_KV), rep(ones, SWA_KV)]).reshape(1, -1)
    one = jnp.ones((HEAD_DIM,), F32)
    zero = jnp.zeros((HEAD_DIM,), F32)
    flag = jnp.concatenate([rep(one, NSA_HEADS + SWA_HEADS), rep(one, NSA_KV), rep(zero, NSA_KV),
                            rep(one, NSA_KV), rep(zero, NSA_KV), rep(one, SWA_KV),
                            rep(zero, SWA_KV)]).reshape(1, -1)
    return wa, wb, gain, flag


def _head_gain_t(gain, n_groups, hpg):
    gt = gain.astype(F32).reshape(n_groups, hpg, HEAD_DIM).transpose(0, 2, 1)
    return jnp.repeat(gt, QBLK, axis=2)


def _mixer(x2, b, t, norm_g, sc, sh, gm, w_in, conv_dw_w, conv_dw_b, conv_ln_g, conv_ln_b, conv_pw_w, cmp_pe,
           cmp_w1, cmp_b1, cmp_w2, nsa_q_g, nsa_k_g, swa_q_g, swa_k_g, swa_sinks, grp_out_g, w_out):
    n, d = x2.shape
    wa, wb, gain, flag = _split_w_in(w_in, nsa_q_g, nsa_k_g, swa_q_g, swa_k_g)
    dummy = jnp.zeros((1, PA_WIDTH), F32)
    pa = _inproj(x2, norm_g, sc, sh, wa, dummy, dummy, rows_per_batch=t, tm=512, tn=PA_WIDTH,
                 out_dtype=F32, epilogue=False, name="inproj_f32")
    pb = _inproj(x2, norm_g, sc, sh, wb, gain, flag, rows_per_batch=t, tm=1024, tn=512,
                 out_dtype=BF16, epilogue=True, name="inproj_bf16")
    pa3 = pa.reshape(b, t, PA_WIDTH)
    pb3 = pb.reshape(b, t, PB_WIDTH)
    og_a = grp_out_g[:CONV_CH]
    og_b = grp_out_g[CONV_CH:CONV_CH + NSA_HEADS * HEAD_DIM]
    og_c = grp_out_g[CONV_CH + NSA_HEADS * HEAD_DIM:]

    y_a = _conformer_conv(pa3, conv_dw_w, conv_dw_b, conv_ln_g, conv_ln_b, conv_pw_w, og_a)

    k_cmp = _compress(pa3, PA_KC, cmp_pe[0], cmp_w1[0], cmp_b1[0], cmp_w2[0], nsa_k_g[0],
                      normalise=True, transpose_out=False)
    v_cmp_t = _compress(pa3, PA_VC, cmp_pe[1], cmp_w1[1], cmp_b1[1], cmp_w2[1], nsa_k_g[0],
                        normalise=False, transpose_out=True)
    ka_slc, vt_slc = _kv_prep(pb3, PB_KS // HEAD_DIM, PB_VS // HEAD_DIM, augment_keys=True)
    _, vt_win = _kv_prep(pb3, PB_KW // HEAD_DIM, PB_VW // HEAD_DIM, augment_keys=False)
    _, vt_swa = _kv_prep(pb3, PB_KC // HEAD_DIM, PB_VC // HEAD_DIM, augment_keys=False)
    o_cmp, qsel = _nsa_cmp(pb3, k_cmp, v_cmp_t, t)
    o_slc = _nsa_slc(qsel, ka_slc, vt_slc, t)
    y_b = _nsa_finish(qsel, pb3, PB_KW // HEAD_DIM, vt_win, o_cmp, o_slc, pa, PA_GATE // HEAD_DIM,
                      _head_gain_t(og_b, NSA_KV, NSA_HPG), t)

    sink_rows = jnp.repeat((swa_sinks.astype(F32) * LOG2E).reshape(SWA_KV, SWA_HPG), QBLK, axis=1).reshape(
        SWA_KV, 1, SWA_HPG * QBLK)
    y_c = _swa(pb3, PB_QS // (SWA_HPG * HEAD_DIM), PB_KC // HEAD_DIM, vt_swa, sink_rows,
               _head_gain_t(og_c, SWA_KV, SWA_HPG), t)

    return _outproj(y_a.reshape(n, -1), y_b.reshape(n, -1), y_c.reshape(n, -1), w_out, x2, gm, rows_per_batch=t)


def kernel(x, c, norm_mix_g, norm_ffn_g, w_ada, b_ada, w_in, conv_dw_w, conv_dw_b, conv_ln_g, conv_ln_b, conv_pw_w, cmp_pe, cmp_w1, cmp_b1, cmp_w2, nsa_q_g, nsa_k_g, swa_q_g, swa_k_g, swa_sinks, grp_out_g, w_out, ffn_w_gate, ffn_w_up, ffn_w_down, moe_router, moe_w_gate, moe_w_up, moe_w_down):
    b, t, d = x.shape
    depth = w_in.shape[0]
    mod = _adaln(c, w_ada, b_ada)
    x2 = x.reshape(b * t, d)
    for i in range(depth):
        sh_m, sc_m, g_m, sh_f, sc_f, g_f = [mod[i, :, k, :].reshape(b, 1, d) for k in range(N_ADA)]
        x2 = _mixer(x2, b, t, norm_mix_g[i].reshape(1, d), sc_m, sh_m, g_m, w_in[i], conv_dw_w[i], conv_dw_b[i],
                    conv_ln_g[i], conv_ln_b[i], conv_pw_w[i], cmp_pe[i], cmp_w1[i], cmp_b1[i], cmp_w2[i],
                    nsa_q_g[i], nsa_k_g[i], swa_q_g[i], swa_k_g[i], swa_sinks[i], grp_out_g[i], w_out[i])
        j = i // 2
        gf = norm_ffn_g[i].reshape(1, d)
        if i % 2 == 0:
            x2 = _ffn_dense(x2, gf, sc_f, sh_f, g_f, ffn_w_gate[j].astype(BF16), ffn_w_up[j].astype(BF16),
                            ffn_w_down[j].astype(BF16), rows_per_batch=t)
        else:
            x2 = _moe(x2, gf, sc_f, sh_f, g_f, moe_router[j], moe_w_gate[j].astype(BF16),
                      moe_w_up[j].astype(BF16), moe_w_down[j].astype(BF16), rows_per_batch=t)
    return x2.reshape(b, t, d)
```

```python
import functools
import math

import jax
import jax.numpy as jnp
from jax import lax
from jax.experimental import pallas as pl
from jax.experimental.pallas import tpu as pltpu

F32 = jnp.float32
BF16 = jnp.bfloat16
HIGHEST = lax.Precision.HIGHEST

HEAD_DIM = 128
CONV_CH = 512
CONV_K = 31
NSA_HEADS = 6
NSA_KV = 2
NSA_HPG = NSA_HEADS // NSA_KV
CMP_BLK = 32
CMP_STRIDE = 16
CMP_HIDDEN = 256
SEL_BLK = 64
SEL_TOP = 16
CMP_PER_SEL = SEL_BLK // CMP_STRIDE
NSA_WIN = 512
SWA_HEADS = 6
SWA_KV = 2
SWA_HPG = SWA_HEADS // SWA_KV
SWA_WIN = 128
N_EXPERTS = 8
TOP_K = 2
N_ADA = 6
NORM_EPS = 1e-6
LN_EPS = 1e-5
LOG2E = math.log2(math.e)

LANE = 128
SUBLANE = 8
BF16_ROWS = 16
VMEM_LIMIT = 56 * 1024 * 1024

QBLK = 256
VROWS = HEAD_DIM + BF16_ROWS
MOE_TM = 1024
NEG = -1e30


def _cparams(sem):
    return pltpu.CompilerParams(dimension_semantics=sem, vmem_limit_bytes=VMEM_LIMIT)


def _mod_rmsnorm(x, g, sc, sh):
    ms = jnp.mean(x * x, axis=-1, keepdims=True)
    y = x * lax.rsqrt(ms + NORM_EPS) * g
    return y * (1.0 + sc) + sh


def _adaln_kernel(c_ref, w_ref, b_ref, o_ref):
    c = c_ref[...]
    ca = c * jax.nn.sigmoid(c)
    o_ref[0] = jnp.dot(ca, w_ref[0], precision=HIGHEST, preferred_element_type=F32) + b_ref[0]


def _adaln(c, w_ada, b_ada):
    n_layers, d, width = w_ada.shape
    b = c.shape[0]
    rows = -(-b // SUBLANE) * SUBLANE
    c_pad = jnp.zeros((rows, d), F32).at[:b].set(c)
    tn = 1024
    out = pl.pallas_call(
        _adaln_kernel,
        out_shape=jax.ShapeDtypeStruct((n_layers, rows, width), F32),
        grid=(n_layers, width // tn),
        in_specs=[
            pl.BlockSpec((rows, d), lambda l, j: (0, 0)),
            pl.BlockSpec((1, d, tn), lambda l, j: (l, 0, j)),
            pl.BlockSpec((1, 1, tn), lambda l, j: (l, 0, j)),
        ],
        out_specs=pl.BlockSpec((1, rows, tn), lambda l, j: (l, 0, j)),
        compiler_params=_cparams(("arbitrary", "arbitrary")),
        name="adaln",
    )(c_pad, w_ada, b_ada.reshape(n_layers, 1, width))
    return out[:, :b].reshape(n_layers, b, N_ADA, d)


def _inproj_kernel(x_ref, g_ref, sc_ref, sh_ref, w_ref, gain_ref, flag_ref, o_ref, h_ref, *, epilogue):
    @pl.when(pl.program_id(1) == 0)
    def _():
        h_ref[...] = _mod_rmsnorm(x_ref[...], g_ref[...], sc_ref[0], sh_ref[0]).astype(BF16)

    y = lax.dot_general(h_ref[...], w_ref[...].astype(BF16), (((1,), (1,)), ((), ())),
                        preferred_element_type=F32)
    if epilogue:
        gain = gain_ref[...]
        flag = flag_ref[...]
        for hh in range(y.shape[1] // HEAD_DIM):
            cs = slice(hh * HEAD_DIM, (hh + 1) * HEAD_DIM)
            yh = y[:, cs]
            ms = jnp.mean(yh * yh, axis=-1, keepdims=True)
            yn = yh * lax.rsqrt(ms + NORM_EPS) * gain[:, cs]
            o_ref[:, cs] = jnp.where(flag[:, cs] > 0.0, yn, yh).astype(o_ref.dtype)
    else:
        o_ref[...] = y.astype(o_ref.dtype)


def _inproj(x2, g, sc, sh, w, gain, flag, *, rows_per_batch, tm, tn, out_dtype, epilogue, name):
    n, d = x2.shape
    width = w.shape[0]
    bpb = rows_per_batch // tm
    return pl.pallas_call(
        functools.partial(_inproj_kernel, epilogue=epilogue),
        out_shape=jax.ShapeDtypeStruct((n, width), out_dtype),
        grid=(n // tm, width // tn),
        in_specs=[
            pl.BlockSpec((tm, d), lambda i, j: (i, 0)),
            pl.BlockSpec((1, d), lambda i, j: (0, 0)),
            pl.BlockSpec((1, 1, d), lambda i, j: (i // bpb, 0, 0)),
            pl.BlockSpec((1, 1, d), lambda i, j: (i // bpb, 0, 0)),
            pl.BlockSpec((tn, d), lambda i, j: (j, 0)),
            pl.BlockSpec((1, tn), lambda i, j: (0, j)),
            pl.BlockSpec((1, tn), lambda i, j: (0, j)),
        ],
        out_specs=pl.BlockSpec((tm, tn), lambda i, j: (i, j)),
        scratch_shapes=[pltpu.VMEM((tm, d), BF16)],
        compiler_params=_cparams(("arbitrary", "arbitrary")),
        name=name,
    )(x2, g, sc, sh, w, gain, flag)


CONV_TT = 256
CONV_HALO = 32
CONV_CHUNK = 32


def _conv_kernel(u_ref, halo_ref, wdw_ref, bdw_ref, lng_ref, lnb_ref, wpw_ref, og_ref, o_ref, z_ref, c_ref):
    t = pl.program_id(1)
    u = u_ref[0]
    z_ref[0, CONV_HALO:, :] = u[:, :CONV_CH] * jax.nn.sigmoid(u[:, CONV_CH:])
    hu = halo_ref[0]
    zh = hu[:, :CONV_CH] * jax.nn.sigmoid(hu[:, CONV_CH:])
    z_ref[0, :CONV_HALO, :] = jnp.where(t == 0, 0.0, zh)
    rows = CONV_TT + CONV_HALO - SUBLANE
    for s in range(1, SUBLANE):
        z_ref[s, :rows, :] = z_ref[0, s:s + rows, :]
    wdw = wdw_ref[...]
    off = CONV_HALO - (CONV_K - 1)
    for c in range(CONV_TT // CONV_CHUNK):
        acc = jnp.zeros((CONV_CHUNK, CONV_CH), F32) + bdw_ref[...]
        for k in range(CONV_K):
            lo = c * CONV_CHUNK + (off + k) // SUBLANE * SUBLANE
            acc = acc + wdw[k:k + 1, :] * z_ref[(off + k) % SUBLANE, lo:lo + CONV_CHUNK, :]
        c_ref[c * CONV_CHUNK:(c + 1) * CONV_CHUNK, :] = acc
    y = c_ref[...]
    mu = jnp.mean(y, axis=-1, keepdims=True)
    var = jnp.mean(jnp.square(y - mu), axis=-1, keepdims=True)
    y = (y - mu) * lax.rsqrt(var + LN_EPS) * lng_ref[...] + lnb_ref[...]
    y = y * jax.nn.sigmoid(y)
    y = jnp.dot(y.astype(BF16), wpw_ref[...], preferred_element_type=F32)
    og = og_ref[...]
    for grp in range(CONV_CH // HEAD_DIM):
        cs = slice(grp * HEAD_DIM, (grp + 1) * HEAD_DIM)
        yh = y[:, cs]
        ms = jnp.mean(yh * yh, axis=-1, keepdims=True)
        o_ref[0, :, cs] = (yh * lax.rsqrt(ms + NORM_EPS) * og[:, cs]).astype(o_ref.dtype)


def _conformer_conv(pa, w_dw, b_dw, ln_g, ln_b, w_pw, out_gain):
    b, t, _ = pa.shape
    hb = CONV_TT // CONV_HALO
    row = lambda v: v.reshape(1, -1)
    vec_spec = pl.BlockSpec((1, CONV_CH), lambda bi, ti: (0, 0))
    return pl.pallas_call(
        _conv_kernel,
        out_shape=jax.ShapeDtypeStruct((b, t, CONV_CH), BF16),
        grid=(b, t // CONV_TT),
        in_specs=[
            pl.BlockSpec((1, CONV_TT, 2 * CONV_CH), lambda bi, ti: (bi, ti, 0)),
            pl.BlockSpec((1, CONV_HALO, 2 * CONV_CH), lambda bi, ti: (bi, jnp.maximum(ti * hb - 1, 0), 0)),
            pl.BlockSpec((CONV_K, CONV_CH), lambda bi, ti: (0, 0)),
            vec_spec, vec_spec, vec_spec,
            pl.BlockSpec((CONV_CH, CONV_CH), lambda bi, ti: (0, 0)),
            vec_spec,
        ],
        out_specs=pl.BlockSpec((1, CONV_TT, CONV_CH), lambda bi, ti: (bi, ti, 0)),
        scratch_shapes=[pltpu.VMEM((SUBLANE, CONV_TT + CONV_HALO, CONV_CH), F32),
                        pltpu.VMEM((CONV_TT, CONV_CH), F32)],
        compiler_params=_cparams(("arbitrary", "arbitrary")),
        name="conformer_conv",
    )(pa, pa, w_dw, row(b_dw), row(ln_g), row(ln_b), w_pw.astype(BF16), row(out_gain))


def _compress_kernel(x_ref, pe_ref, w1_ref, b1_ref, w2_ref, kg_ref, o_ref, acc_ref, *, nj, normalise, transpose_out):
    acc_ref[...] = jnp.zeros_like(acc_ref)
    for c in range(CMP_PER_SEL):
        for pp in range(CMP_STRIDE):
            zc = x_ref[0, pl.ds(CMP_STRIDE * c + pp, nj, stride=SEL_BLK), :]
            lo = (zc + pe_ref[0, pp:pp + 1, :]).astype(BF16)
            hi = (zc + pe_ref[0, CMP_STRIDE + pp:CMP_STRIDE + pp + 1, :]).astype(BF16)
            acc_ref[c] += jnp.dot(lo, w1_ref[0, pp], preferred_element_type=F32)
            acc_ref[CMP_PER_SEL + (c - 1) % CMP_PER_SEL] += jnp.dot(hi, w1_ref[0, CMP_STRIDE + pp],
                                                                  preferred_element_type=F32)
    if transpose_out:
        ones_row = lax.broadcasted_iota(jnp.int32, (BF16_ROWS, CMP_PER_SEL * nj), 0) == 0
        o_ref[0, 0, HEAD_DIM:, :] = ones_row.astype(o_ref.dtype)
    for r in range(CMP_PER_SEL):
        second = acc_ref[CMP_PER_SEL + r]
        if r == CMP_PER_SEL - 1:
            second = pltpu.roll(second, nj - 1, 0)
        act = jax.nn.gelu(acc_ref[r] + second + b1_ref[0])
        if transpose_out:
            out = jnp.dot(w2_ref[0], act.T.astype(BF16), preferred_element_type=F32)
            o_ref[0, 0, :HEAD_DIM, r * nj:(r + 1) * nj] = out.astype(o_ref.dtype)
        else:
            out = jnp.dot(act.astype(BF16), w2_ref[0], preferred_element_type=F32)
            if normalise:
                ms = jnp.mean(out * out, axis=-1, keepdims=True)
                out = out * lax.rsqrt(ms + NORM_EPS) * kg_ref[...]
            o_ref[0, 0, r * nj:(r + 1) * nj, :] = out.astype(o_ref.dtype)


def _compress(pa, col0, pe, w1, b1, w2, kgain, *, normalise, transpose_out):
    b, t, _ = pa.shape
    nj = t // SEL_BLK
    ncp = CMP_PER_SEL * nj
    cb0 = col0 // HEAD_DIM
    oshape = (b, NSA_KV, VROWS, ncp) if transpose_out else (b, NSA_KV, ncp, HEAD_DIM)
    assert not (normalise and transpose_out)
    w2_in = (w2.T if transpose_out else w2)[None].astype(BF16)
    return pl.pallas_call(
        functools.partial(_compress_kernel, nj=nj, normalise=normalise, transpose_out=transpose_out),
        out_shape=jax.ShapeDtypeStruct(oshape, BF16),
        grid=(b, NSA_KV),
        in_specs=[
            pl.BlockSpec((1, t, HEAD_DIM), lambda bi, g: (bi, 0, cb0 + g)),
            pl.BlockSpec((1, CMP_BLK, HEAD_DIM), lambda bi, g: (0, 0, 0)),
            pl.BlockSpec((1, CMP_BLK, HEAD_DIM, CMP_HIDDEN), lambda bi, g: (0, 0, 0, 0)),
            pl.BlockSpec((1, 1, CMP_HIDDEN), lambda bi, g: (0, 0, 0)),
            pl.BlockSpec((1,) + w2_in.shape[1:], lambda bi, g: (0, 0, 0)),
            pl.BlockSpec((1, HEAD_DIM), lambda bi, g: (0, 0)),
        ],
        out_specs=pl.BlockSpec((1, 1) + oshape[2:], lambda bi, g: (bi, g, 0, 0)),
        scratch_shapes=[pltpu.VMEM((2 * CMP_PER_SEL, nj, CMP_HIDDEN), F32)],
        compiler_params=_cparams(("arbitrary", "arbitrary")),
        name="nsa_compress_v" if transpose_out else "nsa_compress_k",
    )(pa, pe[None], w1.reshape(1, CMP_BLK, HEAD_DIM, CMP_HIDDEN).astype(BF16), b1.reshape(1, 1, -1),
      w2_in, kgain.reshape(1, -1))


KVP_TT = 512


def _kv_prep_kernel(k_ref, v_ref, ka_ref, vt_ref, *, augment_keys):
    tt = v_ref.shape[1]
    vt_ref[0, 0, :HEAD_DIM, :] = v_ref[0].astype(F32).T.astype(vt_ref.dtype)
    vt_ref[0, 0, HEAD_DIM:, :] = (lax.broadcasted_iota(jnp.int32, (BF16_ROWS, tt), 0) == 0).astype(vt_ref.dtype)
    if augment_keys:
        key = pl.program_id(2) * tt + lax.broadcasted_iota(jnp.int32, (tt, 1), 0)
        lane = lax.broadcasted_iota(jnp.int32, (tt, LANE), 1)
        ka_ref[0, 0, :, :HEAD_DIM] = k_ref[0]
        ka_ref[0, 0, :, HEAD_DIM:] = (lane == (key // SEL_BLK) % LANE).astype(ka_ref.dtype)
    else:
        ka_ref[...] = jnp.zeros_like(ka_ref)


def _kv_prep(pb, k_cb, v_cb, *, augment_keys):
    b, t, _ = pb.shape
    ka_shape = (b, NSA_KV, t, 2 * HEAD_DIM) if augment_keys else (b, NSA_KV, BF16_ROWS, LANE)
    ka_block = (1, 1, KVP_TT, 2 * HEAD_DIM) if augment_keys else (1, 1, BF16_ROWS, LANE)
    ka_map = (lambda bi, g, ti: (bi, g, ti, 0)) if augment_keys else (lambda bi, g, ti: (bi, g, 0, 0))
    return pl.pallas_call(
        functools.partial(_kv_prep_kernel, augment_keys=augment_keys),
        out_shape=(jax.ShapeDtypeStruct(ka_shape, BF16),
                   jax.ShapeDtypeStruct((b, NSA_KV, VROWS, t), BF16)),
        grid=(b, NSA_KV, t // KVP_TT),
        in_specs=[pl.BlockSpec((1, KVP_TT, HEAD_DIM), lambda bi, g, ti: (bi, ti, k_cb + g)),
                  pl.BlockSpec((1, KVP_TT, HEAD_DIM), lambda bi, g, ti: (bi, ti, v_cb + g))],
        out_specs=(pl.BlockSpec(ka_block, ka_map),
                   pl.BlockSpec((1, 1, VROWS, KVP_TT), lambda bi, g, ti: (bi, g, 0, ti))),
        compiler_params=_cparams(("arbitrary", "arbitrary", "arbitrary")),
        name="kv_prep_aug" if augment_keys else "kv_prep",
    )(pb, pb)


def _query_positions(start, nq):
    col = lax.broadcasted_iota(jnp.int32, (1, NSA_HPG * nq), 1)
    return start + col % nq


def _transpose_queries(q):
    qf = q.astype(F32)
    return jnp.concatenate([qf[:, h * HEAD_DIM:(h + 1) * HEAD_DIM].T for h in range(NSA_HPG)], axis=1)


def _normed_heads_out(o_ref, y, gain, nq):
    ms = jnp.mean(y * y, axis=0, keepdims=True)
    y = y * lax.rsqrt(ms + NORM_EPS) * gain
    for h in range(NSA_HPG):
        o_ref[0, :, h * HEAD_DIM:(h + 1) * HEAD_DIM] = y[:, h * nq:(h + 1) * nq].T.astype(o_ref.dtype)


def _nsa_cmp_kernel(q_ref, kc_ref, vct_ref, o_ref, qs_ref, *, nq, nj, nselp, n_top):
    start = pl.program_id(2) * nq
    qt = _transpose_queries(q_ref[0]).astype(BF16)
    s = jnp.dot(kc_ref[0, 0], qt, preferred_element_type=F32)
    ncp = CMP_PER_SEL * nj
    row = lax.broadcasted_iota(jnp.int32, (ncp, 1), 0)
    cmp_end = SEL_BLK * (row % nj) + CMP_STRIDE * (row // nj) + (CMP_BLK - 1)
    tpos = _query_positions(start, nq)
    s = jnp.where(cmp_end <= tpos, s, NEG)
    m = jnp.max(s, axis=0, keepdims=True)
    m = jnp.where(m <= 0.5 * NEG, 0.0, m)
    p = jnp.exp2(s - m)
    pv = jnp.dot(vct_ref[0, 0], p.astype(BF16), preferred_element_type=F32)
    inv = 1.0 / jnp.maximum(pv[HEAD_DIM:HEAD_DIM + 1, :], 1e-30)
    o_ref[0, 0, 0] = pv[:HEAD_DIM, :] * inv

    imp = p[:, 0:nq] * inv[:, 0:nq]
    for h in range(1, NSA_HPG):
        imp = imp + p[:, h * nq:(h + 1) * nq] * inv[:, h * nq:(h + 1) * nq]
    last = imp[(CMP_PER_SEL - 1) * nj:, :]
    jrow = lax.broadcasted_iota(jnp.int32, (nj, 1), 0)
    prev = jnp.where(jrow == 0, 0.0, pltpu.roll(last, 1, 0))
    imp_sel = prev + last
    for r in range(CMP_PER_SEL - 1):
        imp_sel = imp_sel + imp[r * nj:(r + 1) * nj, :]

    cur = tpos[:, :nq] // SEL_BLK
    behind = lax.bitcast_convert_type(cur - jrow, jnp.uint32)
    selneg = jnp.where(behind <= 1, 0.0, NEG)
    selneg = jnp.where(jrow == 0, 0.0, selneg)
    score = jnp.where(jrow <= cur, imp_sel, NEG)
    score = jnp.where(selneg == 0.0, 2.0 * NEG, score)
    for _ in range(max(n_top - 3, 0)):
        mx = jnp.max(score, axis=0, keepdims=True)
        idx = jnp.min(jnp.where(score == mx, jrow, nj), axis=0, keepdims=True)
        idx = jnp.where(mx > 0.5 * NEG, idx, -1)
        hit = jrow == idx
        selneg = jnp.where(hit, 0.0, selneg)
        score = jnp.where(hit, 2.0 * NEG, score)
    if nselp > nj:
        selneg = jnp.concatenate([selneg, jnp.full((nselp - nj, nq), NEG, F32)], axis=0)
    selneg = jnp.concatenate([selneg] * NSA_HPG, axis=1).astype(BF16)
    for hf in range(nselp // LANE):
        qs_ref[0, 0, 0, 2 * LANE * hf:2 * LANE * hf + LANE, :] = qt
        qs_ref[0, 0, 0, 2 * LANE * hf + LANE:2 * LANE * (hf + 1), :] = selneg[hf * LANE:(hf + 1) * LANE, :]


def _nsa_cmp(pb, kc, vct, t):
    b = pb.shape[0]
    nq = QBLK
    nqb = t // nq
    nj = t // SEL_BLK
    nselp = -(-nj // LANE) * LANE
    n_top = min(SEL_TOP, nj)
    ncp = CMP_PER_SEL * nj
    return pl.pallas_call(
        functools.partial(_nsa_cmp_kernel, nq=nq, nj=nj, nselp=nselp, n_top=n_top),
        out_shape=(jax.ShapeDtypeStruct((b, NSA_KV, nqb, HEAD_DIM, NSA_HPG * nq), F32),
                   jax.ShapeDtypeStruct((b, NSA_KV, nqb, 2 * nselp, NSA_HPG * nq), BF16)),
        grid=(b, NSA_KV, nqb),
        in_specs=[
            pl.BlockSpec((1, nq, NSA_HPG * HEAD_DIM), lambda bi, g, qi: (bi, qi, g)),
            pl.BlockSpec((1, 1, ncp, HEAD_DIM), lambda bi, g, qi: (bi, g, 0, 0)),
            pl.BlockSpec((1, 1, VROWS, ncp), lambda bi, g, qi: (bi, g, 0, 0)),
        ],
        out_specs=(pl.BlockSpec((1, 1, 1, HEAD_DIM, NSA_HPG * nq), lambda bi, g, qi: (bi, g, qi, 0, 0)),
                   pl.BlockSpec((1, 1, 1, 2 * nselp, NSA_HPG * nq), lambda bi, g, qi: (bi, g, qi, 0, 0))),
        compiler_params=_cparams(("arbitrary", "arbitrary", "arbitrary")),
        name="nsa_compressed_topk",
    )(pb, kc, vct)


SLC_QB = 2
KVT = 256


def _nsa_slc_kernel(qs_ref, ka_ref, vt_ref, o_ref, acc_ref, sa_ref, sb_ref, p_ref, m_ref, *, nq):
    qi = pl.program_id(2)
    span = SLC_QB * nq
    start = qi * span
    bcol = NSA_HPG * nq
    ncol = SLC_QB * bcol
    tiles_per_half = LANE * SEL_BLK // KVT
    n_diag = span // KVT
    n_past = qi * n_diag
    krow = lax.broadcasted_iota(jnp.int32, (KVT, 1), 0)
    lane = lax.broadcasted_iota(jnp.int32, (1, LANE), 1)
    acc_ref[...] = jnp.zeros_like(acc_ref)
    m_ref[...] = jnp.full_like(m_ref, NEG)

    def scores(ti, dst_ref):
        k0 = pl.multiple_of(ti * KVT, KVT)
        r0 = pl.multiple_of((ti // tiles_per_half) * 2 * LANE, 2 * LANE)
        ka = ka_ref[0, 0, pl.ds(k0, KVT), :]
        for qb in range(SLC_QB):
            dst_ref[:, qb * bcol:(qb + 1) * bcol] = jnp.dot(ka, qs_ref[0, 0, qb, pl.ds(r0, 2 * LANE), :],
                                                            preferred_element_type=F32)

    def softmax_pv(src_ref, ti, diagonal):
        k0 = pl.multiple_of(ti * KVT, KVT)
        for c in range(ncol // LANE):
            cs = slice(c * LANE, (c + 1) * LANE)
            s = src_ref[:, cs]
            if diagonal:
                q0 = (c * LANE // bcol) * nq + (c * LANE) % nq
                s = jnp.where(k0 + krow <= start + q0 + lane, s, NEG)
            m = m_ref[:, cs]
            mn = jnp.maximum(m, jnp.max(s, axis=0, keepdims=True))
            p_ref[:, cs] = jnp.exp2(s - mn).astype(BF16)
            m_ref[:, cs] = mn
            acc_ref[:, cs] = acc_ref[:, cs] * jnp.exp2(m - mn)
        acc_ref[...] += jnp.dot(vt_ref[0, 0, :, pl.ds(k0, KVT)], p_ref[...], preferred_element_type=F32)

    scores(0, sa_ref)

    def pair(pi, carry):
        t0 = 2 * pi
        scores(t0 + 1, sb_ref)
        softmax_pv(sa_ref, t0, False)
        scores(t0 + 2, sa_ref)
        softmax_pv(sb_ref, t0 + 1, False)
        return carry

    lax.fori_loop(0, n_past // 2, pair, 0)
    bufs = (sa_ref, sb_ref)
    for dt in range(n_diag):
        if dt + 1 < n_diag:
            scores(n_past + dt + 1, bufs[(dt + 1) % 2])
        softmax_pv(bufs[dt % 2], n_past + dt, True)

    for qb in range(SLC_QB):
        cs = slice(qb * bcol, (qb + 1) * bcol)
        o_ref[0, 0, qb] = acc_ref[:HEAD_DIM, cs] / acc_ref[HEAD_DIM:HEAD_DIM + 1, cs]


def _nsa_slc(qsel, ka, vt, t):
    b = qsel.shape[0]
    nq = QBLK
    nqb = t // nq
    rows = qsel.shape[3]
    bcol = NSA_HPG * nq
    ncol = SLC_QB * bcol
    assert nqb % SLC_QB == 0 and (SLC_QB * nq // KVT) % 2 == 0
    return pl.pallas_call(
        functools.partial(_nsa_slc_kernel, nq=nq),
        out_shape=jax.ShapeDtypeStruct((b, NSA_KV, nqb, HEAD_DIM, bcol), F32),
        grid=(b, NSA_KV, nqb // SLC_QB),
        in_specs=[
            pl.BlockSpec((1, 1, SLC_QB, rows, bcol), lambda bi, g, qi: (bi, g, qi, 0, 0)),
            pl.BlockSpec((1, 1, t, 2 * HEAD_DIM), lambda bi, g, qi: (bi, g, 0, 0)),
            pl.BlockSpec((1, 1, VROWS, t), lambda bi, g, qi: (bi, g, 0, 0)),
        ],
        out_specs=pl.BlockSpec((1, 1, SLC_QB, HEAD_DIM, bcol), lambda bi, g, qi: (bi, g, qi, 0, 0)),
        scratch_shapes=[pltpu.VMEM((VROWS, ncol), F32), pltpu.VMEM((KVT, ncol), F32),
                        pltpu.VMEM((KVT, ncol), F32), pltpu.VMEM((KVT, ncol), BF16),
                        pltpu.VMEM((1, ncol), F32)],
        compiler_params=_cparams(("arbitrary", "arbitrary", "arbitrary")),
        name="nsa_selected",
    )(qsel, ka, vt)


def _band_scores(qt, k_ref, vt_ref, start, nq, nk, window, t_len, sink):
    k0 = pl.multiple_of(jnp.clip(start + nq - nk, 0, t_len - nk), LANE)
    s = jnp.dot(k_ref[0, pl.ds(k0, nk), :], qt, preferred_element_type=F32)
    rel = _query_positions(start, nq) - (k0 + lax.broadcasted_iota(jnp.int32, (nk, 1), 0))
    s = jnp.where(lax.bitcast_convert_type(rel, jnp.uint32) < window, s, NEG)
    m = jnp.max(s, axis=0, keepdims=True)
    if sink is not None:
        m = jnp.maximum(m, sink)
    p = jnp.exp2(s - m).astype(BF16)
    pv = jnp.dot(vt_ref[0, 0, :, pl.ds(k0, nk)], p, preferred_element_type=F32)
    den = pv[HEAD_DIM:HEAD_DIM + 1, :]
    if sink is not None:
        den = den + jnp.exp2(sink - m)
    return pv[:HEAD_DIM, :] / den


def _band_keys(window):
    return -(-(window - 1) // LANE) * LANE + QBLK


def _nsa_finish_kernel(qs_ref, k_ref, vt_ref, oc_ref, os_ref, gl_ref, gain_ref, o_ref, gt_ref, *, nq, t_len):
    g = pl.program_id(1)
    start = pl.program_id(2) * nq
    o_win = _band_scores(qs_ref[0, 0, 0], k_ref, vt_ref, start, nq, _band_keys(NSA_WIN), NSA_WIN, t_len, None)
    gt_ref[...] = jax.nn.sigmoid(gl_ref[...]).T
    branch = []
    for br in range(3):
        rows = [gt_ref[pl.ds((g * NSA_HPG + h) * 3 + br, 1), :] for h in range(NSA_HPG)]
        branch.append(jnp.concatenate(rows, axis=1))
    y = branch[0] * oc_ref[0, 0, 0] + branch[1] * os_ref[0, 0, 0] + branch[2] * o_win
    _normed_heads_out(o_ref, y, gain_ref[0], nq)


def _nsa_finish(qsel, pb, k_cb, vt, o_cmp, o_slc, pa2, gate_cb, gain_t, t):
    b = pb.shape[0]
    nq = QBLK
    nqb = t // nq
    ncol = NSA_HPG * nq
    blk5 = pl.BlockSpec((1, 1, 1, HEAD_DIM, ncol), lambda bi, g, qi: (bi, g, qi, 0, 0))
    return pl.pallas_call(
        functools.partial(_nsa_finish_kernel, nq=nq, t_len=t),
        out_shape=jax.ShapeDtypeStruct((b, t, NSA_HEADS * HEAD_DIM), BF16),
        grid=(b, NSA_KV, nqb),
        in_specs=[
            blk5,
            pl.BlockSpec((1, t, HEAD_DIM), lambda bi, g, qi: (bi, 0, k_cb + g)),
            pl.BlockSpec((1, 1, VROWS, t), lambda bi, g, qi: (bi, g, 0, 0)),
            blk5, blk5,
            pl.BlockSpec((nq, LANE), lambda bi, g, qi: (bi * nqb + qi, gate_cb)),
            pl.BlockSpec((1, HEAD_DIM, ncol), lambda bi, g, qi: (g, 0, 0)),
        ],
        out_specs=pl.BlockSpec((1, nq, NSA_HPG * HEAD_DIM), lambda bi, g, qi: (bi, qi, g)),
        scratch_shapes=[pltpu.VMEM((LANE, nq), F32)],
        compiler_params=_cparams(("arbitrary", "arbitrary", "arbitrary")),
        name="nsa_window_finish",
    )(qsel, pb, vt, o_cmp, o_slc, pa2, gain_t)


def _swa_kernel(q_ref, k_ref, vt_ref, sink_ref, gain_ref, o_ref, *, nq, t_len):
    start = pl.program_id(2) * nq
    qt = _transpose_queries(q_ref[0]).astype(BF16)
    y = _band_scores(qt, k_ref, vt_ref, start, nq, _band_keys(SWA_WIN), SWA_WIN, t_len, sink_ref[0])
    _normed_heads_out(o_ref, y, gain_ref[0], nq)


def _swa(pb, q_cb, k_cb, vt, sink_rows, gain_t, t):
    b = pb.shape[0]
    nq = QBLK
    nqb = t // nq
    ncol = SWA_HPG * nq
    return pl.pallas_call(
        functools.partial(_swa_kernel, nq=nq, t_len=t),
        out_shape=jax.ShapeDtypeStruct((b, t, SWA_HEADS * HEAD_DIM), BF16),
        grid=(b, SWA_KV, nqb),
        in_specs=[
            pl.BlockSpec((1, nq, SWA_HPG * HEAD_DIM), lambda bi, g, qi: (bi, qi, q_cb + g)),
            pl.BlockSpec((1, t, HEAD_DIM), lambda bi, g, qi: (bi, 0, k_cb + g)),
            pl.BlockSpec((1, 1, VROWS, t), lambda bi, g, qi: (bi, g, 0, 0)),
            pl.BlockSpec((1, 1, ncol), lambda bi, g, qi: (g, 0, 0)),
            pl.BlockSpec((1, HEAD_DIM, ncol), lambda bi, g, qi: (g, 0, 0)),
        ],
        out_specs=pl.BlockSpec((1, nq, SWA_HPG * HEAD_DIM), lambda bi, g, qi: (bi, qi, g)),
        compiler_params=_cparams(("arbitrary", "arbitrary", "arbitrary")),
        name="swa_sink",
    )(pb, pb, vt, sink_rows, gain_t)


def _outproj_kernel(ya_ref, yb_ref, yc_ref, wa_ref, wb_ref, wc_ref, x_ref, gm_ref, o_ref):
    y = jnp.dot(ya_ref[...], wa_ref[...], preferred_element_type=F32)
    y = y + jnp.dot(yb_ref[...], wb_ref[...], preferred_element_type=F32)
    y = y + jnp.dot(yc_ref[...], wc_ref[...], preferred_element_type=F32)
    o_ref[...] = x_ref[...] + gm_ref[0] * y


def _outproj(ya, yb, yc, w_out, x2, gm, *, rows_per_batch):
    n, d = x2.shape
    tm, tn = 1024, 512
    bpb = rows_per_batch // tm
    wa, wb, wc = ya.shape[1], yb.shape[1], yc.shape[1]
    w = w_out.astype(BF16)
    return pl.pallas_call(
        _outproj_kernel,
        out_shape=jax.ShapeDtypeStruct((n, d), F32),
        grid=(n // tm, d // tn),
        in_specs=[
            pl.BlockSpec((tm, wa), lambda i, j: (i, 0)),
            pl.BlockSpec((tm, wb), lambda i, j: (i, 0)),
            pl.BlockSpec((tm, wc), lambda i, j: (i, 0)),
            pl.BlockSpec((wa, tn), lambda i, j: (0, j)),
            pl.BlockSpec((wb, tn), lambda i, j: (0, j)),
            pl.BlockSpec((wc, tn), lambda i, j: (0, j)),
            pl.BlockSpec((tm, tn), lambda i, j: (i, j)),
            pl.BlockSpec((1, 1, tn), lambda i, j: (i // bpb, 0, j)),
        ],
        out_specs=pl.BlockSpec((tm, tn), lambda i, j: (i, j)),
        compiler_params=_cparams(("arbitrary", "arbitrary")),
        name="mix_outproj",
    )(ya, yb, yc, w[:wa], w[wa:wa + wb], w[wa + wb:], x2, gm)


def _ffn_kernel(x_ref, g_ref, sc_ref, sh_ref, gf_ref, wg_ref, wu_ref, wd_ref, o_ref, h_ref):
    f = pl.program_id(1)

    @pl.when(f == 0)
    def _():
        h_ref[...] = _mod_rmsnorm(x_ref[...], g_ref[...], sc_ref[0], sh_ref[0]).astype(BF16)
        o_ref[...] = jnp.zeros_like(o_ref)

    h = h_ref[...]
    a = jnp.dot(h, wg_ref[...], preferred_element_type=F32)
    u = jnp.dot(h, wu_ref[...], preferred_element_type=F32)
    act = (a * jax.nn.sigmoid(a) * u).astype(BF16)
    o_ref[...] += jnp.dot(act, wd_ref[...], preferred_element_type=F32)

    @pl.when(f == pl.num_programs(1) - 1)
    def _():
        o_ref[...] = x_ref[...] + gf_ref[0] * o_ref[...]


def _ffn_dense(x2, g, sc, sh, gf, wg, wu, wd, *, rows_per_batch):
    n, d = x2.shape
    ff = wg.shape[1]
    tm, tf = 1024, 256
    bpb = rows_per_batch // tm
    mod_spec = pl.BlockSpec((1, 1, d), lambda i, f: (i // bpb, 0, 0))
    return pl.pallas_call(
        _ffn_kernel,
        out_shape=jax.ShapeDtypeStruct((n, d), F32),
        grid=(n // tm, ff // tf),
        in_specs=[
            pl.BlockSpec((tm, d), lambda i, f: (i, 0)),
            pl.BlockSpec((1, d), lambda i, f: (0, 0)),
            mod_spec, mod_spec, mod_spec,
            pl.BlockSpec((d, tf), lambda i, f: (0, f)),
            pl.BlockSpec((d, tf), lambda i, f: (0, f)),
            pl.BlockSpec((tf, d), lambda i, f: (f, 0)),
        ],
        out_specs=pl.BlockSpec((tm, d), lambda i, f: (i, 0)),
        scratch_shapes=[pltpu.VMEM((tm, d), BF16)],
        compiler_params=_cparams(("arbitrary", "arbitrary")),
        name="ffn_dense",
    )(x2, g, sc, sh, gf, wg, wu, wd)


def _router_kernel(x_ref, g_ref, sc_ref, sh_ref, wr_ref, h_ref, idx_ref, wt_ref):
    h = _mod_rmsnorm(x_ref[...], g_ref[...], sc_ref[0], sh_ref[0])
    h_ref[...] = h
    logits = jnp.dot(h, wr_ref[...], precision=HIGHEST, preferred_element_type=F32)
    e = lax.broadcasted_iota(jnp.int32, logits.shape, 1)
    m1 = jnp.max(logits, axis=-1, keepdims=True)
    i1 = jnp.min(jnp.where(logits == m1, e, N_EXPERTS), axis=-1, keepdims=True)
    rest = jnp.where(e == i1, -jnp.inf, logits)
    m2 = jnp.max(rest, axis=-1, keepdims=True)
    i2 = jnp.min(jnp.where(rest == m2, e, N_EXPERTS), axis=-1, keepdims=True)
    e2 = jnp.exp(m2 - m1)
    den = 1.0 + e2
    idx_ref[:, 0:1] = i1
    idx_ref[:, 1:2] = i2
    wt_ref[:, 0:1] = 1.0 / den
    wt_ref[:, 1:2] = e2 / den


def _router(x2, g, sc, sh, w_router, *, rows_per_batch):
    n, d = x2.shape
    tm = 512
    bpb = rows_per_batch // tm
    mod_spec = pl.BlockSpec((1, 1, d), lambda i: (i // bpb, 0, 0))
    return pl.pallas_call(
        _router_kernel,
        out_shape=(jax.ShapeDtypeStruct((n, d), F32),
                   jax.ShapeDtypeStruct((n, TOP_K), jnp.int32),
                   jax.ShapeDtypeStruct((n, TOP_K), F32)),
        grid=(n // tm,),
        in_specs=[
            pl.BlockSpec((tm, d), lambda i: (i, 0)),
            pl.BlockSpec((1, d), lambda i: (0, 0)),
            mod_spec, mod_spec,
            pl.BlockSpec((d, N_EXPERTS), lambda i: (0, 0)),
        ],
        out_specs=(pl.BlockSpec((tm, d), lambda i: (i, 0)),
                   pl.BlockSpec((tm, TOP_K), lambda i: (i, 0)),
                   pl.BlockSpec((tm, TOP_K), lambda i: (i, 0))),
        compiler_params=_cparams(("arbitrary",)),
        name="moe_router",
    )(x2, g, sc, sh, w_router)


DMA_UNROLL = 8


def _prefetch_rows(n_f):
    step = SUBLANE // math.gcd(SUBLANE, n_f)
    return -(-(-(-MOE_TM // n_f)) // step) * step


def _expert_kernel(be_ref, src_ref, nused_ref, h_hbm, wg_ref, wu_ref, wd_ref, o_ref, xg_ref, xb_ref, sem,
                   *, n_blocks, n_f):
    i = pl.program_id(0)
    f = pl.program_id(1)
    used = i < nused_ref[0]
    pf_rows = _prefetch_rows(n_f)
    pf_total = pf_rows * n_f

    def row_copy(blk, r):
        src = src_ref[blk * MOE_TM + jnp.minimum(r, MOE_TM - 1)]
        return pltpu.make_async_copy(h_hbm.at[pl.ds(src, 1), :], xg_ref.at[pl.ds(r, 1), :], sem)

    def wait_gather():
        pltpu.make_async_copy(h_hbm.at[pl.ds(0, pf_total), :], xg_ref.at[pl.ds(0, pf_total), :], sem).wait()

    def prefetch_chunk():
        nxt = jnp.minimum(i + 1, n_blocks - 1)
        for j in range(pf_rows):
            row_copy(nxt, f * pf_rows + j).start()

    @pl.when((i == 0) & (f == 0))
    def _():
        lax.fori_loop(0, pf_total, lambda r, c: (row_copy(0, r).start(), c)[1], 0, unroll=DMA_UNROLL)

    @pl.when(f == 0)
    def _():
        wait_gather()
        xb_ref[...] = xg_ref[:MOE_TM, :].astype(BF16)
        o_ref[...] = jnp.zeros_like(o_ref)

    @pl.when(used)
    def _():
        prefetch_chunk()
        xb = xb_ref[...]
        a = jnp.dot(xb, wg_ref[0], preferred_element_type=F32)
        u = jnp.dot(xb, wu_ref[0], preferred_element_type=F32)
        act = (a * jax.nn.sigmoid(a) * u).astype(BF16)
        o_ref[...] += jnp.dot(act, wd_ref[0], preferred_element_type=F32)

    @pl.when(jnp.logical_not(used))
    def _():
        prefetch_chunk()

    @pl.when((i == n_blocks - 1) & (f == n_f - 1))
    def _():
        wait_gather()


def _experts(h_tok, block_expert, src_row, n_used, wg, wu, wd, n_blocks):
    n, d = h_tok.shape
    ff = wg.shape[2]
    tf = 512
    n_f = ff // tf
    xg_rows = _prefetch_rows(n_f) * n_f
    return pl.pallas_call(
        functools.partial(_expert_kernel, n_blocks=n_blocks, n_f=n_f),
        out_shape=jax.ShapeDtypeStruct((n_blocks * MOE_TM, d), F32),
        grid_spec=pltpu.PrefetchScalarGridSpec(
            num_scalar_prefetch=3,
            grid=(n_blocks, n_f),
            in_specs=[
                pl.BlockSpec(memory_space=pl.ANY),
                pl.BlockSpec((1, d, tf), lambda i, f, be, src, nu: (be[i], 0, f)),
                pl.BlockSpec((1, d, tf), lambda i, f, be, src, nu: (be[i], 0, f)),
                pl.BlockSpec((1, tf, d), lambda i, f, be, src, nu: (be[i], f, 0)),
            ],
            out_specs=pl.BlockSpec((MOE_TM, d), lambda i, f, be, src, nu: (i, 0)),
            scratch_shapes=[pltpu.VMEM((xg_rows, d), F32), pltpu.VMEM((MOE_TM, d), BF16),
                            pltpu.SemaphoreType.DMA],
        ),
        compiler_params=_cparams(("arbitrary", "arbitrary")),
        name="moe_experts",
    )(block_expert, src_row, n_used, h_tok, wg, wu, wd)


CMB_TM = 256


def _combine_kernel(pos_ref, y_hbm, x_ref, wt_ref, gf_ref, o_ref, yb_ref, sem):
    i = pl.program_id(0)

    def row_copy(r, k):
        return pltpu.make_async_copy(y_hbm.at[pl.ds(pos_ref[(i * CMB_TM + r) * TOP_K + k], 1), :],
                                     yb_ref.at[k, pl.ds(r, 1), :], sem)

    def start(r, c):
        for k in range(TOP_K):
            row_copy(r, k).start()
        return c

    def wait(r, c):
        for k in range(TOP_K):
            row_copy(r, k).wait()
        return c

    lax.fori_loop(0, CMB_TM, start, 0, unroll=DMA_UNROLL)
    lax.fori_loop(0, CMB_TM, wait, 0, unroll=DMA_UNROLL)
    wt = wt_ref[...]
    f = yb_ref[0] * wt[:, 0:1]
    for k in range(1, TOP_K):
        f = f + yb_ref[k] * wt[:, k:k + 1]
    o_ref[...] = x_ref[...] + gf_ref[0] * f


def _combine(pos, y_buf, x2, wt, gf, *, rows_per_batch):
    n, d = x2.shape
    bpb = rows_per_batch // CMB_TM
    return pl.pallas_call(
        _combine_kernel,
        out_shape=jax.ShapeDtypeStruct((n, d), F32),
        grid_spec=pltpu.PrefetchScalarGridSpec(
            num_scalar_prefetch=1,
            grid=(n // CMB_TM,),
            in_specs=[
                pl.BlockSpec(memory_space=pl.ANY),
                pl.BlockSpec((CMB_TM, d), lambda i, pos: (i, 0)),
                pl.BlockSpec((CMB_TM, TOP_K), lambda i, pos: (i, 0)),
                pl.BlockSpec((1, 1, d), lambda i, pos: (i // bpb, 0, 0)),
            ],
            out_specs=pl.BlockSpec((CMB_TM, d), lambda i, pos: (i, 0)),
            scratch_shapes=[pltpu.VMEM((TOP_K, CMB_TM, d), F32), pltpu.SemaphoreType.DMA],
        ),
        compiler_params=_cparams(("arbitrary",)),
        name="moe_combine",
    )(pos, y_buf, x2, wt, gf)


def _moe(x2, g, sc, sh, gf, w_router, wg, wu, wd, *, rows_per_batch):
    n, d = x2.shape
    h_tok, top_idx, top_w = _router(x2, g, sc, sh, w_router, rows_per_batch=rows_per_batch)
    n_assign = n * TOP_K
    n_blocks = n_assign // MOE_TM + N_EXPERTS
    expert_of = top_idx.reshape(-1)
    onehot = (expert_of[:, None] == jnp.arange(N_EXPERTS)[None, :]).astype(jnp.int32)
    rank = jnp.take_along_axis(jnp.cumsum(onehot, axis=0) - onehot, expert_of[:, None], axis=1)[:, 0]
    counts = jnp.sum(onehot, axis=0)
    padded = (counts + MOE_TM - 1) // MOE_TM * MOE_TM
    pad_ends = jnp.cumsum(padded)
    pad_starts = pad_ends - padded
    dest = pad_starts[expert_of] + rank
    token_of = jnp.arange(n_assign, dtype=jnp.int32) // TOP_K
    src_row = jnp.zeros((n_blocks * MOE_TM,), jnp.int32).at[dest].set(token_of)
    block_start = jnp.arange(n_blocks, dtype=jnp.int32) * MOE_TM
    block_expert = jnp.minimum(jnp.sum(block_start[:, None] >= pad_ends[None, :], axis=1),
                               N_EXPERTS - 1).astype(jnp.int32)
    n_used = (pad_ends[-1] // MOE_TM).astype(jnp.int32).reshape(1)
    y_buf = _experts(h_tok, block_expert, src_row, n_used, wg, wu, wd, n_blocks)
    return _combine(dest.astype(jnp.int32), y_buf, x2, top_w, gf, rows_per_batch=rows_per_batch)


PA_KC = 2 * CONV_CH
PA_VC = PA_KC + NSA_KV * HEAD_DIM
PA_GATE = PA_VC + NSA_KV * HEAD_DIM
PA_WIDTH = PA_GATE + LANE
PB_QN = 0
PB_QS = NSA_HEADS * HEAD_DIM
PB_KS = PB_QS + SWA_HEADS * HEAD_DIM
PB_VS = PB_KS + NSA_KV * HEAD_DIM
PB_KW = PB_VS + NSA_KV * HEAD_DIM
PB_VW = PB_KW + NSA_KV * HEAD_DIM
PB_KC = PB_VW + NSA_KV * HEAD_DIM
PB_VC = PB_KC + SWA_KV * HEAD_DIM
PB_WIDTH = PB_VC + SWA_KV * HEAD_DIM


def _split_w_in(w_in, nsa_q_g, nsa_k_g, swa_q_g, swa_k_g):
    sizes = [2 * CONV_CH, NSA_HEADS * HEAD_DIM] + [NSA_KV * HEAD_DIM] * 6 + [
        3 * NSA_HEADS, SWA_HEADS * HEAD_DIM, SWA_KV * HEAD_DIM, SWA_KV * HEAD_DIM]
    offs = [0]
    for s in sizes:
        offs.append(offs[-1] + s)
    w_t = w_in.T
    seg = [w_t[offs[i]:offs[i + 1]] for i in range(len(sizes))]
    (u_conv, q_n, kc, vc, ks, vs, kw, vw, g_n, q_s, k_s, v_s) = seg
    d = w_in.shape[0]
    g_pad = jnp.zeros((LANE - 3 * NSA_HEADS, d), w_in.dtype)
    wa = jnp.concatenate([u_conv, kc, vc, g_n, g_pad], axis=0).astype(BF16)
    wb = jnp.concatenate([q_n, q_s, ks, vs, kw, vw, k_s, v_s], axis=0)
    qscale = HEAD_DIM ** -0.5 * LOG2E
    ones = jnp.ones((HEAD_DIM,), F32)

    def rep(v, n):
        return jnp.tile(v.astype(F32), n)

    gain = jnp.concatenate([rep(nsa_q_g * qscale, NSA_HEADS), rep(swa_q_g * qscale, SWA_HEADS),
                            rep(nsa_k_g[1], NSA_KV), rep(ones, NSA_KV), rep(nsa_k_g[2], NSA_KV),
                            rep(ones, NSA_KV), rep(swa_k_g, SWA_KV), rep(ones, SWA_KV)]).reshape(1, -1)
    one = jnp.ones((HEAD_DIM,), F32)
    zero = jnp.zeros((HEAD_DIM,), F32)
    flag = jnp.concatenate([rep(one, NSA_HEADS + SWA_HEADS), rep(one, NSA_KV), rep(zero, NSA_KV),
                            rep(one, NSA_KV), rep(zero, NSA_KV), rep(one, SWA_KV),
                            rep(zero, SWA_KV)]).reshape(1, -1)
    return wa, wb, gain, flag


def _head_gain_t(gain, n_groups, hpg):
    gt = gain.astype(F32).reshape(n_groups, hpg, HEAD_DIM).transpose(0, 2, 1)
    return jnp.repeat(gt, QBLK, axis=2)


def _mixer(x2, b, t, norm_g, sc, sh, gm, w_in, conv_dw_w, conv_dw_b, conv_ln_g, conv_ln_b, conv_pw_w, cmp_pe,
           cmp_w1, cmp_b1, cmp_w2, nsa_q_g, nsa_k_g, swa_q_g, swa_k_g, swa_sinks, grp_out_g, w_out):
    n, d = x2.shape
    wa, wb, gain, flag = _split_w_in(w_in, nsa_q_g, nsa_k_g, swa_q_g, swa_k_g)
    dummy = jnp.zeros((1, PA_WIDTH), F32)
    pa = _inproj(x2, norm_g, sc, sh, wa, dummy, dummy, rows_per_batch=t, tm=512, tn=PA_WIDTH,
                 out_dtype=F32, epilogue=False, name="inproj_f32")
    pb = _inproj(x2, norm_g, sc, sh, wb, gain, flag, rows_per_batch=t, tm=1024, tn=512,
                 out_dtype=BF16, epilogue=True, name="inproj_bf16")
    pa3 = pa.reshape(b, t, PA_WIDTH)
    pb3 = pb.reshape(b, t, PB_WIDTH)
    og_a = grp_out_g[:CONV_CH]
    og_b = grp_out_g[CONV_CH:CONV_CH + NSA_HEADS * HEAD_DIM]
    og_c = grp_out_g[CONV_CH + NSA_HEADS * HEAD_DIM:]

    y_a = _conformer_conv(pa3, conv_dw_w, conv_dw_b, conv_ln_g, conv_ln_b, conv_pw_w, og_a)

    k_cmp = _compress(pa3, PA_KC, cmp_pe[0], cmp_w1[0], cmp_b1[0], cmp_w2[0], nsa_k_g[0],
                      normalise=True, transpose_out=False)
    v_cmp_t = _compress(pa3, PA_VC, cmp_pe[1], cmp_w1[1], cmp_b1[1], cmp_w2[1], nsa_k_g[0],
                        normalise=False, transpose_out=True)
    ka_slc, vt_slc = _kv_prep(pb3, PB_KS // HEAD_DIM, PB_VS // HEAD_DIM, augment_keys=True)
    _, vt_win = _kv_prep(pb3, PB_KW // HEAD_DIM, PB_VW // HEAD_DIM, augment_keys=False)
    _, vt_swa = _kv_prep(pb3, PB_KC // HEAD_DIM, PB_VC // HEAD_DIM, augment_keys=False)
    o_cmp, qsel = _nsa_cmp(pb3, k_cmp, v_cmp_t, t)
    o_slc = _nsa_slc(qsel, ka_slc, vt_slc, t)
    y_b = _nsa_finish(qsel, pb3, PB_KW // HEAD_DIM, vt_win, o_cmp, o_slc, pa, PA_GATE // HEAD_DIM,
                      _head_gain_t(og_b, NSA_KV, NSA_HPG), t)

    sink_rows = jnp.repeat((swa_sinks.astype(F32) * LOG2E).reshape(SWA_KV, SWA_HPG), QBLK, axis=1).reshape(
        SWA_KV, 1, SWA_HPG * QBLK)
    y_c = _swa(pb3, PB_QS // (SWA_HPG * HEAD_DIM), PB_KC // HEAD_DIM, vt_swa, sink_rows,
               _head_gain_t(og_c, SWA_KV, SWA_HPG), t)

    return _outproj(y_a.reshape(n, -1), y_b.reshape(n, -1), y_c.reshape(n, -1), w_out, x2, gm, rows_per_batch=t)


def kernel(x, c, norm_mix_g, norm_ffn_g, w_ada, b_ada, w_in, conv_dw_w, conv_dw_b, conv_ln_g, conv_ln_b, conv_pw_w, cmp_pe, cmp_w1, cmp_b1, cmp_w2, nsa_q_g, nsa_k_g, swa_q_g, swa_k_g, swa_sinks, grp_out_g, w_out, ffn_w_gate, ffn_w_up, ffn_w_down, moe_router, moe_w_gate, moe_w_up, moe_w_down):
    b, t, d = x.shape
    depth = w_in.shape[0]
    mod = _adaln(c, w_ada, b_ada)
    x2 = x.reshape(b * t, d)
    for i in range(depth):
        sh_m, sc_m, g_m, sh_f, sc_f, g_f = [mod[i, :, k, :].reshape(b, 1, d) for k in range(N_ADA)]
        x2 = _mixer(x2, b, t, norm_mix_g[i].reshape(1, d), sc_m, sh_m, g_m, w_in[i], conv_dw_w[i], conv_dw_b[i],
                    conv_ln_g[i], conv_ln_b[i], conv_pw_w[i], cmp_pe[i], cmp_w1[i], cmp_b1[i], cmp_w2[i],
                    nsa_q_g[i], nsa_k_g[i], swa_q_g[i], swa_k_g[i], swa_sinks[i], grp_out_g[i], w_out[i])
        j = i // 2
        gf = norm_ffn_g[i].reshape(1, d)
        if i % 2 == 0:
            x2 = _ffn_dense(x2, gf, sc_f, sh_f, g_f, ffn_w_gate[j].astype(BF16), ffn_w_up[j].astype(BF16),
                            ffn_w_down[j].astype(BF16), rows_per_batch=t)
        else:
            x2 = _moe(x2, gf, sc_f, sh_f, g_f, moe_router[j], moe_w_gate[j].astype(BF16),
                      moe_w_up[j].astype(BF16), moe_w_down[j].astype(BF16), rows_per_batch=t)
    return x2.reshape(b, t, d)
```
